```python
import jax
import jax.numpy as jnp
from jax import lax
import numpy as np

D_MODEL = 1024
BATCH = 16
SEQ = 2048
DEPTH = 2

GRID_W = 64
CTX_LEN = 256
HEAD_DIM = 64
MIX_HEADS = 4
MIX_W = MIX_HEADS * HEAD_DIM
N_BRANCH = 4
RWKV_LORA_W = 64
RWKV_LORA_A = 64
RWKV_LORA_G = 128
RWKV_GN_EPS = HEAD_DIM * 1e-5
RET_CHUNK = 128
HGRN_CHUNK = 64
HGRN_LB_FLOOR = 1e-20
ATTN_KV_HEADS = 2
ATTN_BLOCK = 128
ROPE_BASE = 10000.0
N_EXPERTS = 16
D_EXPERT = 1024
EC_CAPACITY = 2
LN_EPS = 1e-5
RMS_EPS = 1e-6
DEEPNORM_ALPHA = (2 * DEPTH) ** 0.25
DEEPNORM_BETA = (8 * DEPTH) ** -0.25

RWKV_COLS = (MIX_W, MIX_W, MIX_W, 2 * RWKV_LORA_W, 2 * RWKV_LORA_A, RWKV_LORA_G)
RET_COLS = (MIX_W, MIX_W, MIX_W, MIX_W)
HGRN_COLS = (MIX_W, 2 * MIX_W, MIX_W, MIX_W)
ATTN_COLS = (MIX_W, ATTN_KV_HEADS * HEAD_DIM, ATTN_KV_HEADS * HEAD_DIM)
GROUP_COLS = (sum(RWKV_COLS), sum(RET_COLS), sum(HGRN_COLS), sum(ATTN_COLS))
D_IN = sum(GROUP_COLS)

kernel_name = 'hybrid_bidir_diffusion_block'


def split_cols(z, sizes):
    return jnp.split(z, [int(s) for s in np.cumsum(sizes)[:-1]], axis=-1)


def heads(t):
    return t.reshape(t.shape[:-1] + (-1, HEAD_DIM))


def merge_heads(t):
    return t.reshape(t.shape[:-2] + (-1,))


def layer_norm(x, g, b):
    xf = x.astype(jnp.float32)
    mu = jnp.mean(xf, -1, keepdims=True)
    var = jnp.mean(jnp.square(xf - mu), -1, keepdims=True)
    return ((xf - mu) * lax.rsqrt(var + LN_EPS)).astype(x.dtype) * g + b


def rms_norm(x, g):
    xf = x.astype(jnp.float32)
    return (xf * lax.rsqrt(jnp.mean(xf * xf, -1, keepdims=True) + RMS_EPS)).astype(x.dtype) * g


def group_norm(y, g, b, eps):
    yf = y.astype(jnp.float32)
    mu = jnp.mean(yf, -1, keepdims=True)
    var = jnp.mean(jnp.square(yf - mu), -1, keepdims=True)
    return merge_heads(((yf - mu) * lax.rsqrt(var + eps)).astype(y.dtype)) * g + b


def rev_segments(t, n_ctx):
    return jnp.concatenate([jnp.flip(t[:, :n_ctx], 1), jnp.flip(t[:, n_ctx:], 1)], axis=1)


def both_dirs(t):
    return jnp.stack([t, t])


def orient(t, n_ctx):
    return jnp.stack([t[0], rev_segments(t[1], n_ctx)])


def deorient_sum(y, n_ctx):
    return y[0] + rev_segments(y[1], n_ctx)


def centred_shift(t, n_ctx):
    def nb(s):
        p = jnp.pad(s, ((0, 0), (1, 1), (0, 0)))
        return 0.5 * (p[:, :-2] + p[:, 2:])
    return jnp.concatenate([nb(t[:, :n_ctx]), nb(t[:, n_ctx:])], axis=1)


def axial_rope_tables(n_tokens, dtype):
    rows = n_tokens // GRID_W
    row = jnp.repeat(jnp.arange(rows), GRID_W)
    col = jnp.tile(jnp.arange(GRID_W), rows)
    n_freq = HEAD_DIM // 4
    inv = ROPE_BASE ** (-jnp.arange(n_freq, dtype=jnp.float32) / n_freq)
    ang = jnp.concatenate([row[:, None] * inv, col[:, None] * inv], axis=-1)
    return jnp.cos(ang).astype(dtype), jnp.sin(ang).astype(dtype)


def apply_rope(x, cos, sin):
    x1, x2 = x[..., :HEAD_DIM // 2], x[..., HEAD_DIM // 2:]
    c, s = cos[:, None], sin[:, None]
    return jnp.concatenate([x1 * c - x2 * s, x1 * s + x2 * c], axis=-1)


def rope_latent(t, n_ctx, cos, sin):
    return jnp.concatenate([t[:, :n_ctx], apply_rope(t[:, n_ctx:], cos, sin)], axis=1)


def rwkv7_scan(r, w, k, v, kk, a):
    def step(S, inp):
        r_t, w_t, k_t, v_t, kk_t, a_t = inp
        s_kk = jnp.einsum('zbhvk,zbhk->zbhv', S, kk_t)
        S = (S * w_t[..., None, :] - s_kk[..., None] * (kk_t * a_t)[..., None, :]
             + v_t[..., None] * k_t[..., None, :]).astype(S.dtype)
        return S, jnp.einsum('zbhvk,zbhk->zbhv', S, r_t)
    nz, bsz, _, nh, n = r.shape
    S0 = jnp.zeros((nz, bsz, nh, n, n), v.dtype)
    _, ys = lax.scan(step, S0, tuple(jnp.moveaxis(t, 2, 0) for t in (r, w, k, v, kk, a)))
    return jnp.moveaxis(ys, 0, 2)


def rwkv7_mixer(z, n_ctx, lo, mu, w0, w2, a0, a2, g2, k_k, k_a, r_k, ln_g, ln_b):
    z = z + mu * (centred_shift(z, n_ctx) - z)
    r, k, v, zw, za, zg = split_cols(z, RWKV_COLS)
    bsz, n_tok, _ = z.shape
    zw = zw.reshape(bsz, n_tok, 2, RWKV_LORA_W)
    za = za.reshape(bsz, n_tok, 2, RWKV_LORA_A)
    w = w0[:, None, None] + jnp.einsum('btzr,zrc->zbtc', jnp.tanh(zw), w2)
    decay = jnp.exp(-jnp.exp(-jax.nn.softplus(-w) - 0.5))
    a = jax.nn.sigmoid(a0[:, None, None] + jnp.einsum('btzr,zrc->zbtc', za, a2))
    kkf = heads(k * k_k).astype(jnp.float32)
    kk = (kkf * lax.rsqrt(jnp.sum(kkf * kkf, -1, keepdims=True) + 1e-12)).astype(z.dtype)
    k_dir = k * (1 + (a - 1) * k_a)
    ys = rwkv7_scan(orient(both_dirs(heads(r)), n_ctx), orient(heads(decay), n_ctx), orient(heads(k_dir), n_ctx),
                    orient(both_dirs(heads(v)), n_ctx), orient(both_dirs(kk), n_ctx), orient(heads(a), n_ctx))
    y = group_norm(deorient_sum(ys, n_ctx)[:, lo:], ln_g, ln_b, RWKV_GN_EPS)
    rh, kh, vh = heads(r[:, lo:]), heads(k[:, lo:]), heads(v[:, lo:])
    bonus = merge_heads(jnp.sum(rh * kh * heads(r_k), -1, keepdims=True) * vh)
    g = jax.nn.sigmoid(zg[:, lo:]) @ g2
    return (y + bonus) * g


def retention_chunkwise(q, k, v, log_gamma, chunk=RET_CHUNK):
    nz, bsz, n_tok, nh, dk = q.shape
    dv = v.shape[-1]
    n_chunk = n_tok // chunk

    def blocks(t):
        return jnp.moveaxis(t.reshape(nz, bsz, n_chunk, chunk, nh, t.shape[-1]), 2, 0)
    pos = jnp.arange(chunk, dtype=jnp.float32)
    rel = pos[:, None] - pos[None, :]
    lg = log_gamma[:, :, None, None]
    intra = jnp.where(rel >= 0, jnp.exp(lg * jnp.maximum(rel, 0.0)), 0.0).astype(q.dtype)
    lgt = log_gamma[:, None, :]
    q_decay = jnp.exp(lgt * (pos[:, None] + 1.0))[:, None, :, :, None].astype(q.dtype)
    k_decay = jnp.exp(lgt * (chunk - 1.0 - pos[:, None]))[:, None, :, :, None].astype(q.dtype)
    chunk_decay = jnp.exp(log_gamma * chunk)[:, None, :, None, None].astype(v.dtype)

    def step(R, inp):
        qc, kc, vc = inp
        s = jnp.einsum('zbthd,zbshd->zbhts', qc, kc) * intra[:, None]
        o = jnp.einsum('zbhts,zbshe->zbthe', s, vc) + jnp.einsum('zbthd,zbhde->zbthe', qc * q_decay, R)
        R = (R * chunk_decay + jnp.einsum('zbshd,zbshe->zbhde', kc * k_decay, vc)).astype(R.dtype)
        return R, o
    R0 = jnp.zeros((nz, bsz, nh, dk, dv), v.dtype)
    _, o = lax.scan(step, R0, (blocks(q), blocks(k), blocks(v)))
    return jnp.moveaxis(o, 0, 2).reshape(nz, bsz, n_tok, nh, dv)


def retention_mixer(z, n_ctx, lo, cos, sin, decay_logit, norm_g, norm_b):
    q, k, v, g = split_cols(z, RET_COLS)
    q = rope_latent(heads(q), n_ctx, cos, sin)
    k = rope_latent(heads(k), n_ctx, cos, sin) * HEAD_DIM ** -0.5
    log_gamma = jax.nn.log_sigmoid(decay_logit.astype(jnp.float32))
    y = retention_chunkwise(orient(both_dirs(q), n_ctx), orient(both_dirs(k), n_ctx),
                            orient(both_dirs(heads(v)), n_ctx), log_gamma)
    y = deorient_sum(y, n_ctx)[:, lo:]
    return group_norm(y, norm_g, norm_b, LN_EPS) * jax.nn.silu(g[:, lo:])


def gla_chunkwise(q, k, v, log_f, chunk=HGRN_CHUNK):
    nz, bsz, n_tok, nh, dk = q.shape
    dv = v.shape[-1]
    n_chunk = n_tok // chunk

    def blocks(t):
        return jnp.moveaxis(t.reshape(nz, bsz, n_chunk, chunk, nh, t.shape[-1]), 2, 0)
    causal = jnp.tril(jnp.ones((chunk, chunk), dtype=bool))

    def step(S, inp):
        qc, kc, vc, lfc = inp
        b = jnp.cumsum(lfc, axis=2)
        rel = b[:, :, :, None] - b[:, :, None, :]
        dec = jnp.where(causal[:, :, None, None], jnp.exp(jnp.minimum(rel, 0.0)), 0.0).astype(qc.dtype)
        A = jnp.einsum('zbthc,zbtshc,zbshc->zbhts', qc, dec, kc)
        o = (jnp.einsum('zbhts,zbshe->zbthe', A, vc)
             + jnp.einsum('zbthc,zbhce->zbthe', qc * jnp.exp(b).astype(qc.dtype), S))
        b_last = b[:, :, -1:]
        S = (S * jnp.exp(b_last[:, :, 0])[..., None].astype(S.dtype)
             + jnp.einsum('zbshc,zbshe->zbhce', kc * jnp.exp(b_last - b).astype(kc.dtype), vc)).astype(S.dtype)
        return S, o
    S0 = jnp.zeros((nz, bsz, nh, dk, dv), v.dtype)
    _, o = lax.scan(step, S0, (blocks(q), blocks(k), blocks(v), blocks(log_f)))
    return jnp.moveaxis(o, 0, 2).reshape(nz, bsz, n_tok, nh, dv)


def hgrn2_mixer(z, n_ctx, lo, lower, norm_g):
    q, f, i, g = split_cols(z, HGRN_COLS)
    bsz, n_tok, _ = z.shape
    f = jnp.moveaxis(f.reshape(bsz, n_tok, 2, MIX_W), 2, 0).astype(jnp.float32)
    lb = lower[:, None, None, :].astype(jnp.float32)
    log_lb = jnp.log(jnp.maximum(lb, HGRN_LB_FLOOR))
    log_f = jnp.logaddexp(jax.nn.log_sigmoid(f), log_lb + jax.nn.log_sigmoid(-f))
    k = ((1.0 - lb) * jax.nn.sigmoid(-f)).astype(z.dtype)
    y = gla_chunkwise(orient(both_dirs(heads(jax.nn.silu(q))), n_ctx), orient(heads(k), n_ctx),
                      orient(both_dirs(heads(i)), n_ctx), orient(heads(log_f), n_ctx))
    y = deorient_sum(y, n_ctx)[:, lo:]
    return merge_heads(rms_norm(y, heads(norm_g))) * jax.nn.silu(g[:, lo:])


def gqa_mixer(z, n_ctx, need_ctx, cos, sin, q_g, k_g):
    q, k, v = split_cols(z, ATTN_COLS)
    q = rope_latent(rms_norm(heads(q), q_g), n_ctx, cos, sin) * HEAD_DIM ** -0.5
    k = rope_latent(rms_norm(heads(k), k_g), n_ctx, cos, sin)
    v = heads(v)
    bsz, n_tok = z.shape[0], z.shape[1]
    n_lat = n_tok - n_ctx
    group = MIX_HEADS // ATTN_KV_HEADS
    q = q.reshape(bsz, n_tok, ATTN_KV_HEADS, group, HEAD_DIM)

    def attend(qb, kb, vb):
        s = jnp.einsum('bqhgd,bkhd->bhgqk', qb, kb).astype(jnp.float32)
        p = jax.nn.softmax(s, axis=-1).astype(vb.dtype)
        return jnp.einsum('bhgqk,bkhd->bqhgd', p, vb)
    q_blocks = jnp.moveaxis(q[:, n_ctx:].reshape(bsz, n_lat // ATTN_BLOCK, ATTN_BLOCK, ATTN_KV_HEADS, group, HEAD_DIM), 1, 0)
    o_lat = lax.map(lambda qb: attend(qb, k, v), q_blocks)
    o_lat = jnp.moveaxis(o_lat, 0, 1).reshape(bsz, n_lat, MIX_W)
    if not need_ctx:
        return o_lat
    o_ctx = attend(q[:, :n_ctx], k[:, :n_ctx], v[:, :n_ctx]).reshape(bsz, n_ctx, MIX_W)
    return jnp.concatenate([o_ctx, o_lat], axis=1)


def expert_choice_ffn(u, w_router, w_e1, w_e3, w_e2):
    bsz, n_tok, _ = u.shape
    cap = EC_CAPACITY * n_tok // N_EXPERTS
    aff = jax.nn.softmax((u @ w_router).astype(jnp.float32), axis=-1)
    gate, idx = lax.top_k(jnp.swapaxes(aff, 1, 2), cap)
    bidx = jnp.arange(bsz)[:, None, None]
    xs = u[bidx, idx]
    h = jax.nn.silu(jnp.einsum('becd,edf->becf', xs, w_e1)) * jnp.einsum('becd,edf->becf', xs, w_e3)
    y = jnp.einsum('becf,efd->becd', h, w_e2) * gate[..., None].astype(u.dtype)
    return jnp.zeros_like(u).at[bidx, idx].add(y)


def trunk_layer(x_ctx, x_lat, c, c_ctx, cos, sin, hgrn_lower, need_ctx, p):
    n_ctx = x_ctx.shape[1]
    lo = 0 if need_ctx else n_ctx
    mod_lat = jax.nn.silu(c) @ p['w_mod'] + p['b_mod']
    mod_ctx = jax.nn.silu(c_ctx) @ p['w_mod'] + p['b_mod']
    sh1, sc1, g1, sh2, sc2, g2 = jnp.split(mod_lat[:, None, :], 6, axis=-1)
    sh1c, sc1c, g1c, sh2c, sc2c, g2c = jnp.split(mod_ctx, 6, axis=-1)
    u = jnp.concatenate([x_ctx * (1 + sc1c) + sh1c, x_lat * (1 + sc1) + sh1], axis=1)
    z = u @ p['w_in']
    z_a, z_b, z_c, z_d = split_cols(z, GROUP_COLS)
    branches = (
        rwkv7_mixer(z_a, n_ctx, lo, p['rwkv_mu'], p['rwkv_w0'], p['rwkv_w2'], p['rwkv_a0'], p['rwkv_a2'],
                    p['rwkv_g2'], p['rwkv_kk'], p['rwkv_ka'], p['rwkv_rk'], p['rwkv_ln_g'], p['rwkv_ln_b']),
        retention_mixer(z_b, n_ctx, lo, cos, sin, p['ret_decay'], p['ret_norm_g'], p['ret_norm_b']),
        hgrn2_mixer(z_c, n_ctx, lo, hgrn_lower, p['hgrn_norm_g']),
        gqa_mixer(z_d, n_ctx, need_ctx, cos, sin, p['attn_q_g'], p['attn_k_g']),
    )
    u_rows = u[:, lo:]
    merged = None
    for i in range(N_BRANCH):
        term = jax.nn.sigmoid(u_rows @ p['w_gate'][i]) * (branches[i] @ p['w_branch'][i])
        merged = term if merged is None else merged + term
    mix = merged @ p['w_out']
    n0 = n_ctx - lo
    x_lat = layer_norm(DEEPNORM_ALPHA * x_lat + g1 * mix[:, n0:], p['ln1_g'], p['ln1_b'])
    ffn_lat = expert_choice_ffn(x_lat * (1 + sc2) + sh2, p['w_router'], p['w_e1'], p['w_e3'], p['w_e2'])
    x_lat = layer_norm(DEEPNORM_ALPHA * x_lat + g2 * ffn_lat, p['ln2_g'], p['ln2_b'])
    if need_ctx:
        x_ctx = layer_norm(DEEPNORM_ALPHA * x_ctx + g1c * mix[:, :n_ctx], p['ln1_g'], p['ln1_b'])
        ffn_ctx = expert_choice_ffn(x_ctx * (1 + sc2c) + sh2c, p['w_router'], p['w_e1'], p['w_e3'], p['w_e2'])
        x_ctx = layer_norm(DEEPNORM_ALPHA * x_ctx + g2c * ffn_ctx, p['ln2_g'], p['ln2_b'])
    return x_ctx, x_lat


def setup_inputs(seed: int = 0) -> dict:
    key = jax.random.key(seed)
    ks = iter(jax.random.split(key, 40))

    def nrm(shape, scale):
        return scale * jax.random.normal(next(ks), shape, jnp.float32)

    def unif(shape, lo, hi):
        return jax.random.uniform(next(ks), shape, jnp.float32, lo, hi)
    L, D = DEPTH, D_MODEL
    ret_base = jnp.log(2.0 ** (5.0 + jnp.arange(MIX_HEADS, dtype=jnp.float32)) - 1.0)
    return {
        'x': nrm((BATCH, SEQ, D), 1.0),
        'c': nrm((BATCH, D), 1.0),
        'ctx': nrm((BATCH, CTX_LEN, D), 1.0),
        'c_ctx': nrm((D,), 1.0),
        'w_mod': nrm((L, D, 6 * D), 0.5 * D ** -0.5),
        'b_mod': nrm((L, 6 * D), 0.02),
        'w_in': nrm((L, D, D_IN), D ** -0.5),
        'rwkv_mu': unif((L, GROUP_COLS[0]), 0.0, 1.0),
        'rwkv_w0': unif((L, 2, MIX_W), -5.0, 1.0),
        'rwkv_w2': nrm((L, 2, RWKV_LORA_W, MIX_W), 0.1 * RWKV_LORA_W ** -0.5),
        'rwkv_a0': nrm((L, 2, MIX_W), 0.5),
        'rwkv_a2': nrm((L, 2, RWKV_LORA_A, MIX_W), 0.5 * RWKV_LORA_A ** -0.5),
        'rwkv_g2': nrm((L, RWKV_LORA_G, MIX_W), RWKV_LORA_G ** -0.5),
        'rwkv_kk': 0.85 + nrm((L, MIX_W), 0.05),
        'rwkv_ka': 1.0 + nrm((L, MIX_W), 0.05),
        'rwkv_rk': nrm((L, MIX_W), 0.1),
        'rwkv_ln_g': 1.0 + nrm((L, MIX_W), 0.05),
        'rwkv_ln_b': nrm((L, MIX_W), 0.02),
        'ret_decay': ret_base + nrm((L, 2, MIX_HEADS), 0.1),
        'ret_norm_g': 1.0 + nrm((L, MIX_W), 0.05),
        'ret_norm_b': nrm((L, MIX_W), 0.02),
        'hgrn_lb': 1.0 + nrm((2, L, MIX_W), 0.1),
        'hgrn_norm_g': 1.0 + nrm((L, MIX_W), 0.05),
        'attn_q_g': 1.0 + nrm((L, HEAD_DIM), 0.05),
        'attn_k_g': 1.0 + nrm((L, HEAD_DIM), 0.05),
        'w_gate': nrm((L, N_BRANCH, D, D), D ** -0.5),
        'w_branch': nrm((L, N_BRANCH, MIX_W, D), MIX_W ** -0.5),
        'w_out': nrm((L, D, D), DEEPNORM_BETA * D ** -0.5),
        'ln1_g': 1.0 + nrm((L, D), 0.05),
        'ln1_b': nrm((L, D), 0.02),
        'w_router': nrm((L, D, N_EXPERTS), D ** -0.5),
        'w_e1': nrm((L, N_EXPERTS, D, D_EXPERT), D ** -0.5),
        'w_e3': nrm((L, N_EXPERTS, D, D_EXPERT), D ** -0.5),
        'w_e2': nrm((L, N_EXPERTS, D_EXPERT, D), DEEPNORM_BETA * D_EXPERT ** -0.5),
        'ln2_g': 1.0 + nrm((L, D), 0.05),
        'ln2_b': nrm((L, D), 0.02),
    }


def reference(x, c, ctx, c_ctx, w_mod, b_mod, w_in, rwkv_mu, rwkv_w0, rwkv_w2, rwkv_a0, rwkv_a2, rwkv_g2,
              rwkv_kk, rwkv_ka, rwkv_rk, rwkv_ln_g, rwkv_ln_b, ret_decay, ret_norm_g, ret_norm_b, hgrn_lb,
              hgrn_norm_g, attn_q_g, attn_k_g, w_gate, w_branch, w_out, ln1_g, ln1_b, w_router, w_e1, w_e3,
              w_e2, ln2_g, ln2_b):
    cos, sin = axial_rope_tables(x.shape[1], x.dtype)
    lb_w = jax.nn.softmax(hgrn_lb.astype(jnp.float32), axis=1)
    lower = jnp.cumsum(lb_w, axis=1) - lb_w[:, :1]
    x_ctx, x_lat = ctx, x
    for l in range(DEPTH):
        p = {
            'w_mod': w_mod[l], 'b_mod': b_mod[l], 'w_in': w_in[l],
            'rwkv_mu': rwkv_mu[l], 'rwkv_w0': rwkv_w0[l], 'rwkv_w2': rwkv_w2[l], 'rwkv_a0': rwkv_a0[l],
            'rwkv_a2': rwkv_a2[l], 'rwkv_g2': rwkv_g2[l], 'rwkv_kk': rwkv_kk[l], 'rwkv_ka': rwkv_ka[l],
            'rwkv_rk': rwkv_rk[l], 'rwkv_ln_g': rwkv_ln_g[l], 'rwkv_ln_b': rwkv_ln_b[l],
            'ret_decay': ret_decay[l], 'ret_norm_g': ret_norm_g[l], 'ret_norm_b': ret_norm_b[l],
            'hgrn_norm_g': hgrn_norm_g[l], 'attn_q_g': attn_q_g[l], 'attn_k_g': attn_k_g[l],
            'w_gate': w_gate[l], 'w_branch': w_branch[l], 'w_out': w_out[l], 'ln1_g': ln1_g[l], 'ln1_b': ln1_b[l],
            'w_router': w_router[l], 'w_e1': w_e1[l], 'w_e3': w_e3[l], 'w_e2': w_e2[l],
            'ln2_g': ln2_g[l], 'ln2_b': ln2_b[l],
        }
        x_ctx, x_lat = trunk_layer(x_ctx, x_lat, c, c_ctx, cos, sin, lower[:, l], l < DEPTH - 1, p)
    return x_lat
```

```python
import functools

import jax
import jax.numpy as jnp
import numpy as np
from jax import lax
from jax.experimental import pallas as pl
from jax.experimental.pallas import tpu as pltpu

F32 = jnp.float32
BF16 = jnp.bfloat16

HEAD_DIM = 64
MIX_HEADS = 4
MIX_W = MIX_HEADS * HEAD_DIM
GRID_W = 64
ROPE_BASE = 10000.0
N_EXPERTS = 16
EC_CAPACITY = 2
LN_EPS = 1e-5
RMS_EPS = 1e-6
RWKV_GN_EPS = HEAD_DIM * 1e-5
HGRN_LB_FLOOR = 1e-20
RWKV_LORA = 64
COLS_A, COLS_B, COLS_C, COLS_D = 1152, 1024, 1280, 512
CHUNK = 64
SUB_SHIFT = 4
SUB = 1 << SUB_SHIFT
EXP = MIX_HEADS * CHUNK
ROUTE_BLOCK = 256
VMEM_LIMIT = 56 * 1024 * 1024


def _cp(*sem):
    return pltpu.CompilerParams(dimension_semantics=sem, vmem_limit_bytes=VMEM_LIMIT)


def _mm(a, b):
    return jnp.dot(a.astype(BF16), b.astype(BF16), preferred_element_type=F32)


def _mm_nt(a, b):
    return lax.dot_general(a.astype(BF16), b.astype(BF16), (((1,), (1,)), ((), ())), preferred_element_type=F32)


def _mm_tn(a, b):
    return lax.dot_general(a.astype(BF16), b.astype(BF16), (((0,), (0,)), ((), ())), preferred_element_type=F32)


def _split(x):
    hi = x.astype(BF16)
    lo = (x - hi.astype(F32)).astype(BF16)
    return hi, lo


def _mm_mask_l(mask_bf16, x):
    hi, lo = _split(x)
    return (jnp.dot(mask_bf16, hi, preferred_element_type=F32) + jnp.dot(mask_bf16, lo, preferred_element_type=F32))


def _mm_mask_r(x, mask_bf16):
    hi, lo = _split(x)
    return (jnp.dot(hi, mask_bf16, preferred_element_type=F32) + jnp.dot(lo, mask_bf16, preferred_element_type=F32))


def _iota(shape, dim):
    return lax.broadcasted_iota(jnp.int32, shape, dim)


def _head_ones(n):
    return jnp.where((_iota((n, n), 0) >> 6) == (_iota((n, n), 1) >> 6), 1.0, 0.0).astype(BF16)


def _headsum(x):
    return _mm_mask_r(x, _head_ones(x.shape[-1]))


def _sigmoid(x):
    return 1.0 / (1.0 + jnp.exp(-x))


def _silu(x):
    return x * _sigmoid(x)


def _softplus(x):
    return jnp.maximum(x, 0.0) + jnp.log(1.0 + jnp.exp(-jnp.abs(x)))


def _expand(x):
    xt = jnp.concatenate([x] * MIX_HEADS, axis=0)
    keep = (_iota(xt.shape, 0) >> 6) == (_iota(xt.shape, 1) >> 6)
    return jnp.where(keep, xt, 0.0)


def _tile(x):
    return jnp.concatenate([x] * MIX_HEADS, axis=0)


def _collapse(y):
    return y[0:CHUNK] + y[CHUNK:2 * CHUNK] + y[2 * CHUNK:3 * CHUNK] + y[3 * CHUNK:4 * CHUNK]


def _scan_masks(dirn):
    r, c = _iota((EXP, EXP), 0), _iota((EXP, EXP), 1)
    same = (r >> 6) == (c >> 6)
    t, s = r & (CHUNK - 1), c & (CHUNK - 1)
    if dirn == 1:
        t, s = (CHUNK - 1) - t, (CHUNK - 1) - s
    return same, t, s


def _cum_mask(dirn):
    t, s = _iota((CHUNK, CHUNK), 0), _iota((CHUNK, CHUNK), 1)
    return jnp.where((s <= t) if dirn == 0 else (s >= t), 1.0, 0.0).astype(BF16)


def _rope(x, cos, sin):
    blocks = [x[:, i:i + 128] for i in range(0, x.shape[-1], 128)]
    fwd = jnp.concatenate([pltpu.roll(b, 32, 1) for b in blocks], axis=1)
    bwd = jnp.concatenate([pltpu.roll(b, 128 - 32, 1) for b in blocks], axis=1)
    first = (_iota(x.shape, 1) & 63) < 32
    return x * cos + jnp.where(first, bwd, fwd) * sin


def _mod_kernel(c_ref, w_ref, b_ref, o_ref):
    o_ref[0] = _mm(_silu(c_ref[...]), w_ref[0]) + b_ref[0]


def _modulation(cc, w_mod, b_mod):
    depth, d, d6 = w_mod.shape
    tn = d6 // 4
    rows = cc.shape[0]
    return pl.pallas_call(
        _mod_kernel,
        grid=(depth, d6 // tn),
        in_specs=[pl.BlockSpec((rows, d), lambda l, j: (0, 0)),
                  pl.BlockSpec((1, d, tn), lambda l, j: (l, 0, j)),
                  pl.BlockSpec((1, 1, tn), lambda l, j: (l, 0, j))],
        out_specs=pl.BlockSpec((1, rows, tn), lambda l, j: (l, 0, j)),
        out_shape=jax.ShapeDtypeStruct((depth, rows, d6), F32),
        compiler_params=_cp("arbitrary", "arbitrary"),
        name="modulation",
    )(cc, w_mod, b_mod.reshape(depth, 1, d6))


def _inproj_kernel(x_ref, mod_ref, w_ref, za_ref, zb_ref, zc_ref, zd_ref):
    d = x_ref.shape[-1]
    mod = mod_ref[0, 0]
    u = (x_ref[0] * (1.0 + mod[:, d:2 * d]) + mod[:, 0:d]).astype(BF16)
    o = 0
    for ref, n in ((za_ref, COLS_A), (zb_ref, COLS_B), (zc_ref, COLS_C), (zd_ref, COLS_D)):
        ref[0] = jnp.dot(u, w_ref[:, o:o + n], preferred_element_type=F32)
        o += n


def _in_projection(x_all, modsel, w_in, tm, n_lat_tiles):
    b, t, d = x_all.shape
    d_in = w_in.shape[-1]
    cols = (COLS_A, COLS_B, COLS_C, COLS_D)
    return pl.pallas_call(
        _inproj_kernel,
        grid=(b, t // tm),
        in_specs=[pl.BlockSpec((1, tm, d), lambda i, j: (i, j, 0)),
                  pl.BlockSpec((1, 1, 1, 6 * d), lambda i, j: (i, (j >= n_lat_tiles).astype(jnp.int32), 0, 0)),
                  pl.BlockSpec((d, d_in), lambda i, j: (0, 0))],
        out_specs=[pl.BlockSpec((1, tm, n), lambda i, j: (i, j, 0)) for n in cols],
        out_shape=[jax.ShapeDtypeStruct((b, t, n), F32) for n in cols],
        compiler_params=_cp("arbitrary", "arbitrary"),
        name="in_projection",
    )(x_all, modsel, w_in)


def _chunk_maps(n_lat_chunks, n_chunks):
    n_ctx_chunks = n_chunks - n_lat_chunks

    def fwd(i):
        return jnp.where(i < n_ctx_chunks, n_lat_chunks + i, i - n_ctx_chunks)

    def bwd(i):
        return jnp.where(i < n_ctx_chunks, n_chunks - 1 - i, n_chunks - 1 - i)

    return fwd, bwd


def _rwkv_direction(dirn, zc, prev_row, next_row, s_ref, p):
    rowi = _iota(zc.shape, 0)
    up = jnp.where(rowi == 0, prev_row, pltpu.roll(zc, 1, 0))
    dn = jnp.where(rowi == CHUNK - 1, next_row, pltpu.roll(zc, CHUNK - 1, 0))
    zs = zc + p["mu"] * (0.5 * (up + dn) - zc)
    r, k, v = zs[:, 0:256], zs[:, 256:512], zs[:, 512:768]
    zw = zs[:, 768 + RWKV_LORA * dirn:768 + RWKV_LORA * (dirn + 1)]
    za = zs[:, 896 + RWKV_LORA * dirn:896 + RWKV_LORA * (dirn + 1)]
    w = p["w0"][dirn:dirn + 1] + _mm(jnp.tanh(zw), p["w2"][dirn])
    lw = -jnp.exp(-_softplus(-w) - 0.5)
    a = _sigmoid(p["a0"][dirn:dirn + 1] + _mm(za, p["a2"][dirn]))
    kkf = k * p["k_k"]
    kk = kkf * lax.rsqrt(_headsum(kkf * kkf) + 1e-12)
    kd = k * (1.0 + (a - 1.0) * p["k_a"])
    bv = kk * a

    cum = _mm_mask_l(_cum_mask(dirn), lw)
    total = cum[CHUNK - 1:CHUNK] if dirn == 0 else cum[0:1]
    inv = jnp.exp(-cum)
    tail = jnp.exp(total - cum)
    lhs = jnp.concatenate([_expand(kk * jnp.exp(cum - lw)), _expand(r * jnp.exp(cum))], axis=0)
    rhs = jnp.concatenate([_tile(kd * inv), _tile(bv * inv)], axis=0)
    gram = _mm_nt(lhs, rhs)
    same, t, s = _scan_masks(dirn)
    strict, incl = same & (s < t), same & (s <= t)
    m_k = jnp.where(strict, gram[0:EXP, 0:EXP], 0.0)
    m_b = jnp.where(strict, gram[0:EXP, EXP:2 * EXP], 0.0)
    n_k = jnp.where(incl, gram[EXP:2 * EXP, 0:EXP], 0.0)
    n_b = jnp.where(incl, gram[EXP:2 * EXP, EXP:2 * EXP], 0.0)

    st = s_ref[...]
    carry = _mm_nt(lhs, st)
    v_e = _expand(v)
    x = carry[0:EXP] + _mm(m_k, v_e)
    x = x - _mm(m_b, x)
    pw = _mm(m_b, m_b)
    for step in range(5):
        x = x + _mm(pw, x)
        if step < 4:
            pw = _mm(pw, pw)
    y_e = carry[EXP:2 * EXP] + _mm(n_k, v_e) - _mm(n_b, x)
    s_ref[...] = st * jnp.exp(total) + _mm_tn(v_e, _expand(kd * tail)) - _mm_tn(x, _expand(bv * tail))
    return _collapse(y_e), r, k, v, zs[:, 1024:1152]


def _rwkv_kernel(n_lat_chunks, n_chunks,
                 zf_ref, zfp_ref, zfn_ref, zb_ref, zbp_ref, zbn_ref,
                 mu_ref, w0_ref, w2_ref, a0_ref, a2_ref, g2_ref, kk_ref, ka_ref, rk_ref,
                 y0_ref, y1_ref, bonus_ref, gate_ref, s_ref):
    i = pl.program_id(1)
    fwd, bwd = _chunk_maps(n_lat_chunks, n_chunks)

    @pl.when(i == 0)
    def _():
        s_ref[...] = jnp.zeros_like(s_ref)

    p = {"mu": mu_ref[...], "w0": w0_ref[...], "w2": w2_ref, "a0": a0_ref[...], "a2": a2_ref,
         "k_k": kk_ref[...], "k_a": ka_ref[...]}
    for dirn, (z_ref, zp_ref, zn_ref) in enumerate(((zf_ref, zfp_ref, zfn_ref), (zb_ref, zbp_ref, zbn_ref))):
        c = fwd(i) if dirn == 0 else bwd(i)
        first = jnp.logical_or(c == 0, c == n_lat_chunks)
        last = jnp.logical_or(c == n_lat_chunks - 1, c == n_chunks - 1)
        prev_row = jnp.where(first, 0.0, zp_ref[0][7:8, :])
        next_row = jnp.where(last, 0.0, zn_ref[0][0:1, :])
        y, r, k, v, zg = _rwkv_direction(dirn, z_ref[0], prev_row, next_row, s_ref.at[dirn], p)
        if dirn == 0:
            y0_ref[0] = y
            bonus_ref[0] = _headsum(r * k * rk_ref[...]) * v
            gate_ref[0] = _mm(_sigmoid(zg), g2_ref[...])
        else:
            y1_ref[0] = y


def _rwkv_mixer(z_a, n_lat, prm):
    b, t, _ = z_a.shape
    n_chunks, n_lat_chunks = t // CHUNK, n_lat // CHUNK
    fwd, bwd = _chunk_maps(n_lat_chunks, n_chunks)
    per8 = CHUNK // 8
    last8 = t // 8 - 1

    def zspecs(cm):
        return [pl.BlockSpec((1, CHUNK, COLS_A), lambda bi, i: (bi, cm(i), 0)),
                pl.BlockSpec((1, 8, COLS_A), lambda bi, i: (bi, jnp.maximum(cm(i) * per8 - 1, 0), 0)),
                pl.BlockSpec((1, 8, COLS_A), lambda bi, i: (bi, jnp.minimum((cm(i) + 1) * per8, last8), 0))]

    def full(a):
        return pl.BlockSpec(a.shape, lambda bi, i, _n=a.ndim: (0,) * _n)

    params = [prm["mu"], prm["w0"], prm["w2"], prm["a0"], prm["a2"], prm["g2"], prm["k_k"], prm["k_a"], prm["r_k"]]
    yspec_f = pl.BlockSpec((1, CHUNK, MIX_W), lambda bi, i: (bi, fwd(i), 0))
    yspec_b = pl.BlockSpec((1, CHUNK, MIX_W), lambda bi, i: (bi, bwd(i), 0))
    shape = jax.ShapeDtypeStruct((b, t, MIX_W), F32)
    return pl.pallas_call(
        functools.partial(_rwkv_kernel, n_lat_chunks, n_chunks),
        grid=(b, n_chunks),
        in_specs=zspecs(fwd) + zspecs(bwd) + [full(a) for a in params],
        out_specs=[yspec_f, yspec_b, yspec_f, yspec_f],
        out_shape=[shape] * 4,
        scratch_shapes=[pltpu.VMEM((2, MIX_W, MIX_W), F32)],
        compiler_params=_cp("arbitrary", "arbitrary"),
        name="rwkv7",
    )(z_a, z_a, z_a, z_a, z_a, z_a, *params)


def _ret_direction(dirn, zc, cos, sin, lg, r_ref):
    q = _rope(zc[:, 0:256], cos, sin)
    k = _rope(zc[:, 256:512], cos, sin) * HEAD_DIM ** -0.5
    v_e = _expand(zc[:, 512:768])
    pos = _iota((CHUNK, MIX_W), 0).astype(F32)
    if dirn == 1:
        pos = (CHUNK - 1.0) - pos
    q_e = _expand(q)
    same, t, s = _scan_masks(dirn)
    lg_rows = _tile(jnp.broadcast_to(lg, (CHUNK, MIX_W)))
    head_lane = (_iota((EXP, MIX_W), 0) >> 6) == (_iota((EXP, MIX_W), 1) >> 6)
    lg_col = jnp.sum(jnp.where(head_lane, lg_rows, 0.0), axis=1, keepdims=True) * (1.0 / HEAD_DIM)
    rel = jnp.maximum((t - s).astype(F32), 0.0)
    intra = jnp.where(same & (s <= t), jnp.exp(lg_col * rel), 0.0)
    sc = _mm_nt(q_e, _tile(k)) * intra
    rs = r_ref[...]
    o_e = _mm(sc, v_e) + _mm(_expand(q * jnp.exp(lg * (pos + 1.0))), rs)
    kd_e = _expand(k * jnp.exp(lg * ((CHUNK - 1.0) - pos)))
    chunk_decay = jnp.exp(lg_col * float(CHUNK))
    r_ref[...] = rs * chunk_decay + _mm_tn(kd_e, v_e)
    return _collapse(o_e)


def _ret_kernel(n_lat_chunks, n_chunks, zf_ref, zb_ref, cf_ref, sf_ref, cb_ref, sb_ref, lg_ref,
                y0_ref, y1_ref, r_ref):
    @pl.when(pl.program_id(1) == 0)
    def _():
        r_ref[...] = jnp.zeros_like(r_ref)

    y0_ref[0] = _ret_direction(0, zf_ref[0], cf_ref[...], sf_ref[...], lg_ref[0:1], r_ref.at[0])
    y1_ref[0] = _ret_direction(1, zb_ref[0], cb_ref[...], sb_ref[...], lg_ref[1:2], r_ref.at[1])


def _retention_mixer(z_b, n_lat, cos, sin, lg_lanes):
    b, t, _ = z_b.shape
    n_chunks, n_lat_chunks = t // CHUNK, n_lat // CHUNK
    fwd, bwd = _chunk_maps(n_lat_chunks, n_chunks)
    shape = jax.ShapeDtypeStruct((b, t, MIX_W), F32)

    def zs(cm):
        return pl.BlockSpec((1, CHUNK, COLS_B), lambda bi, i: (bi, cm(i), 0))

    def ts(cm):
        return pl.BlockSpec((CHUNK, MIX_W), lambda bi, i: (cm(i), 0))

    return pl.pallas_call(
        functools.partial(_ret_kernel, n_lat_chunks, n_chunks),
        grid=(b, n_chunks),
        in_specs=[zs(fwd), zs(bwd), ts(fwd), ts(fwd), ts(bwd), ts(bwd),
                  pl.BlockSpec((2, MIX_W), lambda bi, i: (0, 0))],
        out_specs=[pl.BlockSpec((1, CHUNK, MIX_W), lambda bi, i: (bi, fwd(i), 0)),
                   pl.BlockSpec((1, CHUNK, MIX_W), lambda bi, i: (bi, bwd(i), 0))],
        out_shape=[shape] * 2,
        scratch_shapes=[pltpu.VMEM((2, MIX_W, MIX_W), F32)],
        compiler_params=_cp("arbitrary", "arbitrary"),
        name="retention",
    )(z_b, z_b, cos, sin, cos, sin, lg_lanes)


def _hgrn_direction(dirn, zc, lb, log_lb, s_ref):
    qs = _silu(zc[:, 0:256])
    fz = zc[:, 256 + MIX_W * dirn:256 + MIX_W * (dirn + 1)]
    v_e = _expand(zc[:, 768:1024])
    ls_pos, ls_neg = -_softplus(-fz), -_softplus(fz)
    p1, p2 = ls_pos, log_lb + ls_neg
    log_f = jnp.maximum(p1, p2) + jnp.log(1.0 + jnp.exp(-jnp.abs(p1 - p2)))
    kg = (1.0 - lb) * _sigmoid(-fz)
    bc = _mm_mask_l(_cum_mask(dirn), log_f)
    total = bc[CHUNK - 1:CHUNK] if dirn == 0 else bc[0:1]
    excl = bc - log_f

    same, t, s = _scan_masks(dirn)
    nsub = CHUNK // SUB
    a_bd = jnp.zeros((EXP, EXP), F32)
    rowblk = _iota((CHUNK, MIX_W), 0) >> SUB_SHIFT
    for blk in range(nsub):
        first = blk * SUB if dirn == 0 else blk * SUB + SUB - 1
        e_blk = excl[first:first + 1]
        q_s = jnp.where(rowblk == blk, qs * jnp.exp(jnp.minimum(bc - e_blk, 0.0)), 0.0)
        k_s = kg * jnp.exp(jnp.minimum(e_blk - bc, 0.0))
        g = _mm_nt(_expand(q_s), _tile(k_s))
        pb = blk if dirn == 0 else nsub - 1 - blk
        a_bd = a_bd + jnp.where(same & ((s >> SUB_SHIFT) < pb), g, 0.0)
    for dd in range(SUB):
        sh = dd if dirn == 0 else (CHUNK - dd) % CHUNK
        kr = kg if dd == 0 else pltpu.roll(kg, sh, 0)
        br = bc if dd == 0 else pltpu.roll(bc, sh, 0)
        val = _headsum(qs * kr * jnp.exp(jnp.minimum(bc - br, 0.0)))
        pick = same & ((t - s) == dd) & ((t >> SUB_SHIFT) == (s >> SUB_SHIFT))
        a_bd = a_bd + jnp.where(pick, _tile(val), 0.0)

    st = s_ref[...]
    o_e = _mm(a_bd, v_e) + _mm_nt(_expand(qs * jnp.exp(bc)), st)
    s_ref[...] = st * jnp.exp(total) + _mm_tn(v_e, _expand(kg * jnp.exp(total - bc)))
    return _collapse(o_e)


def _hgrn_kernel(zf_ref, zb_ref, lb_ref, llb_ref, y0_ref, y1_ref, s_ref):
    @pl.when(pl.program_id(1) == 0)
    def _():
        s_ref[...] = jnp.zeros_like(s_ref)

    y0_ref[0] = _hgrn_direction(0, zf_ref[0], lb_ref[0:1], llb_ref[0:1], s_ref.at[0])
    y1_ref[0] = _hgrn_direction(1, zb_ref[0], lb_ref[1:2], llb_ref[1:2], s_ref.at[1])


def _hgrn_mixer(z_c, n_lat, lb, log_lb):
    b, t, _ = z_c.shape
    n_chunks, n_lat_chunks = t // CHUNK, n_lat // CHUNK
    fwd, bwd = _chunk_maps(n_lat_chunks, n_chunks)
    shape = jax.ShapeDtypeStruct((b, t, MIX_W), F32)
    return pl.pallas_call(
        _hgrn_kernel,
        grid=(b, n_chunks),
        in_specs=[pl.BlockSpec((1, CHUNK, COLS_C), lambda bi, i: (bi, fwd(i), 0)),
                  pl.BlockSpec((1, CHUNK, COLS_C), lambda bi, i: (bi, bwd(i), 0)),
                  pl.BlockSpec((2, MIX_W), lambda bi, i: (0, 0)),
                  pl.BlockSpec((2, MIX_W), lambda bi, i: (0, 0))],
        out_specs=[pl.BlockSpec((1, CHUNK, MIX_W), lambda bi, i: (bi, fwd(i), 0)),
                   pl.BlockSpec((1, CHUNK, MIX_W), lambda bi, i: (bi, bwd(i), 0))],
        out_shape=[shape] * 2,
        scratch_shapes=[pltpu.VMEM((2, MIX_W, MIX_W), F32)],
        compiler_params=_cp("arbitrary", "arbitrary"),
        name="hgrn2",
    )(z_c, z_c, lb, log_lb)


def _attn_kernel(tq, n_lat, zq_ref, zkv_ref, cos_ref, sin_ref, qg_ref, kg_ref, o_ref, k_s, v_s):
    j = pl.program_id(1)
    t = zkv_ref.shape[1]

    @pl.when(j == 0)
    def _():
        kf = zkv_ref[0][:, 256:384]
        kn = kf * lax.rsqrt(_headsum(kf * kf) * (1.0 / HEAD_DIM) + RMS_EPS) * kg_ref[...]
        k_s[...] = _rope(kn, cos_ref[:, 0:128], sin_ref[:, 0:128]).astype(BF16)
        v_s[...] = zkv_ref[0][:, 384:512].astype(BF16)

    row0 = pl.multiple_of(j * tq, tq)
    qf = zq_ref[0][:, 0:256]
    qn = qf * lax.rsqrt(_headsum(qf * qf) * (1.0 / HEAD_DIM) + RMS_EPS) * qg_ref[...]
    q = (_rope(qn, cos_ref[pl.ds(row0, tq), :], sin_ref[pl.ds(row0, tq), :]) * HEAD_DIM ** -0.5).astype(BF16)
    is_ctx = row0 >= n_lat
    hidden = jnp.logical_and(is_ctx, _iota((tq, t), 1) < n_lat)
    outs = []
    for h in range(MIX_HEADS):
        g = h // 2
        sc = lax.dot_general(q[:, h * 64:(h + 1) * 64], k_s[:, g * 64:(g + 1) * 64], (((1,), (1,)), ((), ())),
                             preferred_element_type=F32)
        sc = jnp.where(hidden, -jnp.inf, sc)
        e = jnp.exp(sc - jnp.max(sc, axis=1, keepdims=True))
        pr = (e / jnp.sum(e, axis=1, keepdims=True)).astype(BF16)
        outs.append(jnp.dot(pr, v_s[:, g * 64:(g + 1) * 64], preferred_element_type=F32))
    o_ref[0] = jnp.concatenate(outs, axis=1)


def _attention_mixer(z_d, n_lat, tq, nq, cos, sin, q_g, k_g):
    b, t, _ = z_d.shape
    return pl.pallas_call(
        functools.partial(_attn_kernel, tq, n_lat),
        grid=(b, nq),
        in_specs=[pl.BlockSpec((1, tq, COLS_D), lambda bi, j: (bi, j, 0)),
                  pl.BlockSpec((1, t, COLS_D), lambda bi, j: (bi, 0, 0)),
                  pl.BlockSpec((t, MIX_W), lambda bi, j: (0, 0)),
                  pl.BlockSpec((t, MIX_W), lambda bi, j: (0, 0)),
                  pl.BlockSpec((1, MIX_W), lambda bi, j: (0, 0)),
                  pl.BlockSpec((1, 128), lambda bi, j: (0, 0))],
        out_specs=pl.BlockSpec((1, tq, MIX_W), lambda bi, j: (bi, j, 0)),
        out_shape=jax.ShapeDtypeStruct((b, nq * tq, MIX_W), F32),
        scratch_shapes=[pltpu.VMEM((t, 128), BF16), pltpu.VMEM((t, 128), BF16)],
        compiler_params=_cp("arbitrary", "arbitrary"),
        name="attention",
    )(z_d, z_d, cos, sin, q_g, k_g)


def _layer_norm(x, g, b):
    mu = jnp.mean(x, axis=-1, keepdims=True)
    xc = x - mu
    var = jnp.mean(xc * xc, axis=-1, keepdims=True)
    return xc * lax.rsqrt(var + LN_EPS) * g + b


def _group_norm(y, eps):
    mu = _headsum(y) * (1.0 / HEAD_DIM)
    yc = y - mu
    var = _headsum(yc * yc) * (1.0 / HEAD_DIM)
    return yc * lax.rsqrt(var + eps)


def _merge_kernel(alpha, x_ref, mod_ref, ry0, ry1, rbonus, rgate, ty0, ty1, tg, hy0, hy1, hg, at_ref,
                  rln_g, rln_b, tn_g, tn_b, hn_g, wg_ref, wb_ref, wo_ref, ln_g, ln_b, wr_ref,
                  x1_ref, u2_ref, aff_ref):
    d = x_ref.shape[-1]
    mod = mod_ref[0, 0]
    x = x_ref[0]
    u = (x * (1.0 + mod[:, d:2 * d]) + mod[:, 0:d]).astype(BF16)
    hy = hy0[0] + hy1[0]
    branches = (
        (_group_norm(ry0[0] + ry1[0], RWKV_GN_EPS) * rln_g[...] + rln_b[...] + rbonus[0]) * rgate[0],
        (_group_norm(ty0[0] + ty1[0], LN_EPS) * tn_g[...] + tn_b[...]) * _silu(tg[0]),
        hy * lax.rsqrt(_headsum(hy * hy) * (1.0 / HEAD_DIM) + RMS_EPS) * hn_g[...] * _silu(hg[0]),
        at_ref[0],
    )
    merged = None
    for i, br in enumerate(branches):
        term = _sigmoid(jnp.dot(u, wg_ref[i], preferred_element_type=F32)) * _mm(br, wb_ref[i])
        merged = term if merged is None else merged + term
    mix = _mm(merged, wo_ref[...])
    x1 = _layer_norm(alpha * x + mod[:, 2 * d:3 * d] * mix, ln_g[...], ln_b[...])
    x1_ref[0] = x1
    u2 = (x1 * (1.0 + mod[:, 4 * d:5 * d]) + mod[:, 3 * d:4 * d]).astype(BF16)
    u2_ref[0] = u2
    logits = lax.dot_general(wr_ref[...], u2, (((1,), (1,)), ((), ())), preferred_element_type=F32)
    e = jnp.exp(logits - jnp.max(logits, axis=0, keepdims=True))
    aff_ref[0] = e / jnp.sum(e, axis=0, keepdims=True)


def _merge(alpha, x_all, modsel, rw, rt, z_b, hg, z_c, att, prm, tm, n_lat_tiles, n_tiles):
    b, _, d = x_all.shape
    rows = n_tiles * tm

    def tile(w):
        return pl.BlockSpec((1, tm, w), lambda i, j: (i, j, 0))

    def colblock(k):
        return pl.BlockSpec((1, tm, MIX_W), lambda i, j: (i, j, k))

    def full(a):
        return pl.BlockSpec(a.shape, lambda i, j, _n=a.ndim: (0,) * _n)

    params = [prm["rwkv_ln_g"], prm["rwkv_ln_b"], prm["ret_norm_g"], prm["ret_norm_b"], prm["hgrn_norm_g"],
              prm["w_gate"], prm["w_branch"], prm["w_out"], prm["ln1_g"], prm["ln1_b"], prm["w_router_t"]]
    return pl.pallas_call(
        functools.partial(_merge_kernel, alpha),
        grid=(b, n_tiles),
        in_specs=[tile(d), pl.BlockSpec((1, 1, 1, 6 * d),
                                        lambda i, j: (i, (j >= n_lat_tiles).astype(jnp.int32), 0, 0))]
        + [tile(MIX_W)] * 4 + [tile(MIX_W)] * 2 + [colblock(3)] + [tile(MIX_W)] * 2 + [colblock(4)] + [tile(MIX_W)]
        + [full(a) for a in params],
        out_specs=[tile(d), tile(d), pl.BlockSpec((1, N_EXPERTS, tm), lambda i, j: (i, 0, j))],
        out_shape=[jax.ShapeDtypeStruct((b, rows, d), F32), jax.ShapeDtypeStruct((b, rows, d), BF16),
                   jax.ShapeDtypeStruct((b, N_EXPERTS, rows), F32)],
        compiler_params=_cp("arbitrary", "arbitrary"),
        name="merge",
    )(x_all, modsel, *rw, rt[0], rt[1], z_b, hg[0], hg[1], z_c, att, *params)


def _route_kernel(start, n, cap, aff_ref, code_ref, gate_ref):
    a = aff_ref[0][:, start:start + n]
    bits = pltpu.bitcast(a, jnp.int32)

    def bisect(_, lohi):
        lo, hi = lohi
        mid = lo + ((hi - lo + 1) >> 1)
        ok = jnp.sum(jnp.where(bits >= mid, 1.0, 0.0), axis=1, keepdims=True) >= cap
        return jnp.where(ok, mid, lo), jnp.where(ok, hi, mid - 1)

    lo0 = jnp.zeros((N_EXPERTS, 1), jnp.int32)
    thr, _ = lax.fori_loop(0, 31, bisect, (lo0, jnp.full((N_EXPERTS, 1), 0x7F800000, jnp.int32)))
    gt, eq = bits > thr, bits == thr
    need = cap - jnp.sum(jnp.where(gt, 1.0, 0.0), axis=1, keepdims=True)
    blk = min(n, ROUTE_BLOCK)
    upper = jnp.where(_iota((blk, blk), 0) <= _iota((blk, blk), 1), 1.0, 0.0).astype(BF16)

    def prefix_count(m):
        parts, running = [], jnp.zeros((N_EXPERTS, 1), F32)
        for o in range(0, n, blk):
            pre = jnp.dot(jnp.where(m[:, o:o + blk], 1.0, 0.0).astype(BF16), upper, preferred_element_type=F32)
            parts.append(pre + running)
            running = running + pre[:, blk - 1:blk]
        return jnp.concatenate(parts, axis=1)

    sel = gt | (eq & (prefix_count(eq) <= need))
    rank = prefix_count(sel) - 1.0
    code_ref[0] = jnp.where(sel, rank.astype(jnp.int32), -1)
    gate_ref[0] = a


def _route(aff, start, n, cap):
    b, e, rows = aff.shape
    code, gate = pl.pallas_call(
        functools.partial(_route_kernel, start, n, cap),
        grid=(b,),
        in_specs=[pl.BlockSpec((1, e, rows), lambda i: (i, 0, 0))],
        out_specs=[pl.BlockSpec((1, e, n), lambda i: (i, 0, 0))] * 2,
        out_shape=[jax.ShapeDtypeStruct((b, e, n), jnp.int32), jax.ShapeDtypeStruct((b, e, n), F32)],
        compiler_params=_cp("arbitrary"),
        name="route",
    )(aff)
    return code.reshape(b, e, 1, n), gate.reshape(b, e, 1, n)


def _moe_kernel(bg, cap_pad, code_ref, gate_ref, u_ref, w1_ref, w3_ref, w2_ref, o_ref):
    @pl.when(pl.program_id(1) == 0)
    def _():
        o_ref[...] = jnp.zeros_like(o_ref)

    n = u_ref.shape[1]
    slot = _iota((cap_pad, n), 0)
    hits = [slot == code_ref[s, 0] for s in range(bg)]
    xs = jnp.concatenate([jnp.dot(jnp.where(h, 1.0, 0.0).astype(BF16), u_ref[s], preferred_element_type=F32)
                          for s, h in enumerate(hits)], axis=0).astype(BF16)
    hmid = _silu(jnp.dot(xs, w1_ref[0], preferred_element_type=F32)) * jnp.dot(xs, w3_ref[0],
                                                                               preferred_element_type=F32)
    y = jnp.dot(hmid.astype(BF16), w2_ref[0], preferred_element_type=F32)
    for s, h in enumerate(hits):
        weights = jnp.where(h, gate_ref[s, 0], 0.0)
        o_ref[s] += _mm_tn(weights, y[s * cap_pad:(s + 1) * cap_pad])


def _moe(code, gate, u2, row_block, n, cap, bg, w1, w3, w2):
    b = u2.shape[0]
    d = u2.shape[-1]
    cap_pad = max(cap, 128)
    f = w1.shape[-1]
    return pl.pallas_call(
        functools.partial(_moe_kernel, bg, cap_pad),
        grid=(b // bg, N_EXPERTS),
        in_specs=[pl.BlockSpec((bg, 1, 1, n), lambda i, e: (i, e, 0, 0)),
                  pl.BlockSpec((bg, 1, 1, n), lambda i, e: (i, e, 0, 0)),
                  pl.BlockSpec((bg, n, d), lambda i, e: (i, row_block, 0)),
                  pl.BlockSpec((1, d, f), lambda i, e: (e, 0, 0)),
                  pl.BlockSpec((1, d, f), lambda i, e: (e, 0, 0)),
                  pl.BlockSpec((1, f, d), lambda i, e: (e, 0, 0))],
        out_specs=pl.BlockSpec((bg, n, d), lambda i, e: (i, 0, 0)),
        out_shape=jax.ShapeDtypeStruct((b, n, d), F32),
        compiler_params=_cp("arbitrary", "arbitrary"),
        name="expert_ffn",
    )(code, gate, u2, w1, w3, w2)


def _ln2_kernel(alpha, x_ref, f_ref, mod_ref, g_ref, b_ref, o_ref):
    d = x_ref.shape[-1]
    o_ref[0] = _layer_norm(alpha * x_ref[0] + mod_ref[0, 0][:, 5 * d:6 * d] * f_ref[0], g_ref[...], b_ref[...])


def _ln2(alpha, x1, x1_tile0, ffn, modsel, seg, g, bias, tm):
    b, rows, d = ffn.shape
    return pl.pallas_call(
        functools.partial(_ln2_kernel, alpha),
        grid=(b, rows // tm),
        in_specs=[pl.BlockSpec((1, tm, d), lambda i, j: (i, j + x1_tile0, 0)),
                  pl.BlockSpec((1, tm, d), lambda i, j: (i, j, 0)),
                  pl.BlockSpec((1, 1, 1, 6 * d), lambda i, j: (i, seg, 0, 0)),
                  pl.BlockSpec((1, d), lambda i, j: (0, 0)),
                  pl.BlockSpec((1, d), lambda i, j: (0, 0))],
        out_specs=pl.BlockSpec((1, tm, d), lambda i, j: (i, j, 0)),
        out_shape=jax.ShapeDtypeStruct((b, rows, d), F32),
        compiler_params=_cp("arbitrary", "arbitrary"),
        name="ln2",
    )(x1, ffn, modsel, g, bias)


def _rope_tables(n_lat, n_ctx):
    rows = n_lat // GRID_W
    row = jnp.repeat(jnp.arange(rows), GRID_W)
    col = jnp.tile(jnp.arange(GRID_W), rows)
    n_freq = HEAD_DIM // 4
    inv = ROPE_BASE ** (-jnp.arange(n_freq, dtype=F32) / n_freq)
    ang = jnp.concatenate([row[:, None] * inv, col[:, None] * inv], axis=-1)
    cos, sin = jnp.cos(ang), jnp.sin(ang)
    cos64 = jnp.concatenate([cos, cos], axis=-1)
    sin64 = jnp.concatenate([-sin, sin], axis=-1)
    cos64 = jnp.concatenate([cos64, jnp.ones((n_ctx, HEAD_DIM), F32)], axis=0)
    sin64 = jnp.concatenate([sin64, jnp.zeros((n_ctx, HEAD_DIM), F32)], axis=0)
    return jnp.tile(cos64, (1, MIX_HEADS)), jnp.tile(sin64, (1, MIX_HEADS))


def kernel(x, c, ctx, c_ctx, w_mod, b_mod, w_in, rwkv_mu, rwkv_w0, rwkv_w2, rwkv_a0, rwkv_a2, rwkv_g2, rwkv_kk, rwkv_ka, rwkv_rk, rwkv_ln_g, rwkv_ln_b, ret_decay, ret_norm_g, ret_norm_b, hgrn_lb, hgrn_norm_g, attn_q_g, attn_k_g, w_gate, w_branch, w_out, ln1_g, ln1_b, w_router, w_e1, w_e3, w_e2, ln2_g, ln2_b):
    bsz, n_lat, d = x.shape
    n_ctx = ctx.shape[1]
    depth = w_mod.shape[0]
    t = n_lat + n_ctx
    tm = min(256, n_ctx)
    assert n_lat % tm == 0 and n_ctx % tm == 0 and tm % CHUNK == 0 and n_lat % n_ctx == 0
    n_lat_tiles, n_tiles = n_lat // tm, t // tm
    alpha = (2 * depth) ** 0.25

    cos, sin = _rope_tables(n_lat, n_ctx)
    lb_w = jax.nn.softmax(hgrn_lb.astype(F32), axis=1)
    lower = jnp.cumsum(lb_w, axis=1) - lb_w[:, :1]
    log_lower = jnp.log(jnp.maximum(lower, HGRN_LB_FLOOR))
    lg_lanes = jnp.repeat(jax.nn.log_sigmoid(ret_decay.astype(F32)), HEAD_DIM, axis=-1)

    rows = 8 * ((bsz + 1 + 7) // 8)
    cc = jnp.zeros((rows, d), F32).at[:bsz].set(c).at[bsz].set(c_ctx)
    mods = _modulation(cc, w_mod.astype(BF16), b_mod)

    x_all = jnp.concatenate([x, ctx], axis=1)
    row = lambda a: a.reshape(1, -1)
    for l in range(depth):
        need_ctx = l < depth - 1
        modsel = jnp.stack([mods[l, :bsz], jnp.broadcast_to(mods[l, bsz], (bsz, 6 * d))],
                           axis=1).reshape(bsz, 2, 1, 6 * d)
        z_a, z_b, z_c, z_d = _in_projection(x_all, modsel, w_in[l].astype(BF16), tm, n_lat_tiles)
        rw = _rwkv_mixer(z_a, n_lat, {
            "mu": row(rwkv_mu[l]), "w0": rwkv_w0[l], "w2": rwkv_w2[l].astype(BF16), "a0": rwkv_a0[l],
            "a2": rwkv_a2[l].astype(BF16), "g2": rwkv_g2[l].astype(BF16), "k_k": row(rwkv_kk[l]),
            "k_a": row(rwkv_ka[l]), "r_k": row(rwkv_rk[l])})
        rt = _retention_mixer(z_b, n_lat, cos, sin, lg_lanes[l])
        hg = _hgrn_mixer(z_c, n_lat, lower[:, l], log_lower[:, l])
        n_out_tiles = n_tiles if need_ctx else n_lat_tiles
        att = _attention_mixer(z_d, n_lat, tm, n_out_tiles, cos, sin,
                               row(jnp.tile(attn_q_g[l], MIX_HEADS)), row(jnp.tile(attn_k_g[l], 2)))
        x1, u2, aff = _merge(alpha, x_all, modsel, rw, rt, z_b, hg, z_c, att, {
            "rwkv_ln_g": row(rwkv_ln_g[l]), "rwkv_ln_b": row(rwkv_ln_b[l]), "ret_norm_g": row(ret_norm_g[l]),
            "ret_norm_b": row(ret_norm_b[l]), "hgrn_norm_g": row(hgrn_norm_g[l]),
            "w_gate": w_gate[l].astype(BF16), "w_branch": w_branch[l].astype(BF16), "w_out": w_out[l].astype(BF16),
            "ln1_g": row(ln1_g[l]), "ln1_b": row(ln1_b[l]), "w_router_t": w_router[l].T.astype(BF16)},
            tm, n_lat_tiles, n_out_tiles)
        w1, w3, w2 = w_e1[l].astype(BF16), w_e3[l].astype(BF16), w_e2[l].astype(BF16)
        cap_lat = EC_CAPACITY * n_lat // N_EXPERTS
        code, gate = _route(aff, 0, n_lat, cap_lat)
        ffn = _moe(code, gate, u2, 0, n_lat, cap_lat, 1, w1, w3, w2)
        x_lat = _ln2(alpha, x1, 0, ffn, modsel, 0, row(ln2_g[l]), row(ln2_b[l]), tm)
        if not need_ctx:
            return x_lat
        cap_ctx = EC_CAPACITY * n_ctx // N_EXPERTS
        code_c, gate_c = _route(aff, n_lat, n_ctx, cap_ctx)
        ffn_c = _moe(code_c, gate_c, u2, n_lat // n_ctx, n_ctx, cap_ctx, 8 if bsz % 8 == 0 else 1, w1, w3, w2)
        x_ctx = _ln2(alpha, x1, n_lat_tiles, ffn_c, modsel, 1, row(ln2_g[l]), row(ln2_b[l]), tm)
        x_all = jnp.concatenate([x_lat, x_ctx], axis=1)
    return x_all[:, :n_lat]
```

```python
import functools

import jax
import jax.numpy as jnp
import numpy as np
from jax import lax
from jax.experimental import pallas as pl
from jax.experimental.pallas import tpu as pltpu

F32 = jnp.float32
BF16 = jnp.bfloat16

HEAD_DIM = 64
MIX_HEADS = 4
MIX_W = MIX_HEADS * HEAD_DIM
GRID_W = 64
ROPE_BASE = 10000.0
N_EXPERTS = 16
EC_CAPACITY = 2
LN_EPS = 1e-5
RMS_EPS = 1e-6
RWKV_GN_EPS = HEAD_DIM * 1e-5
HGRN_LB_FLOOR = 1e-20
LOG2_E = 1.4426950408889634
RWKV_LORA = 64
COLS_A, COLS_B, COLS_C, COLS_D = 1152, 1024, 1280, 512
CHUNK = 64
SUB_SHIFT = 4
SUB = 1 << SUB_SHIFT
EXP = MIX_HEADS * CHUNK
REC_BATCH = 4
ROUTE_BLOCK = 256
VMEM_LIMIT = 56 * 1024 * 1024


def _cp(*sem):
    return pltpu.CompilerParams(dimension_semantics=sem, vmem_limit_bytes=VMEM_LIMIT)


def _mm(a, b):
    return jnp.dot(a.astype(BF16), b.astype(BF16), preferred_element_type=F32)


def _mm_nt(a, b):
    return lax.dot_general(a.astype(BF16), b.astype(BF16), (((1,), (1,)), ((), ())), preferred_element_type=F32)


def _mm_tn(a, b):
    return lax.dot_general(a.astype(BF16), b.astype(BF16), (((0,), (0,)), ((), ())), preferred_element_type=F32)


def _split(x):
    hi = x.astype(BF16)
    lo = (x - hi.astype(F32)).astype(BF16)
    return hi, lo


def _mm_mask_l(mask_bf16, x):
    hi, lo = _split(x)
    return (jnp.dot(mask_bf16, hi, preferred_element_type=F32) + jnp.dot(mask_bf16, lo, preferred_element_type=F32))


def _mm_mask_r(x, mask_bf16):
    hi, lo = _split(x)
    return (jnp.dot(hi, mask_bf16, preferred_element_type=F32) + jnp.dot(lo, mask_bf16, preferred_element_type=F32))


def _iota(shape, dim):
    return lax.broadcasted_iota(jnp.int32, shape, dim)


def _head_ones(n):
    return jnp.where((_iota((n, n), 0) >> 6) == (_iota((n, n), 1) >> 6), 1.0, 0.0).astype(BF16)


def _headsum(x):
    return _mm_mask_r(x, _head_ones(x.shape[-1]))


def _sigmoid(x):
    return 1.0 / (1.0 + jnp.exp(-x))


def _silu(x):
    return x * _sigmoid(x)


def _softplus(x):
    return jnp.maximum(x, 0.0) + jnp.log(1.0 + jnp.exp(-jnp.abs(x)))


def _expand(x, headmask):
    return jnp.concatenate([x] * MIX_HEADS, axis=0) * headmask


def _tile(x):
    return jnp.concatenate([x] * MIX_HEADS, axis=0)


def _collapse(y):
    return y[0:CHUNK] + y[CHUNK:2 * CHUNK] + y[2 * CHUNK:3 * CHUNK] + y[3 * CHUNK:4 * CHUNK]


def _scan_constants():
    rows = np.arange(EXP)
    head = rows // CHUNK
    same = head[:, None] == head[None, :]
    lane_head = np.arange(MIX_W) // HEAD_DIM
    lane_tok = np.arange(MIX_W) % HEAD_DIM
    tri, cum, earlier, pick = [], [], [], []
    for dirn in (0, 1):
        flip = (lambda a: a) if dirn == 0 else (lambda a: CHUNK - 1 - a)
        pos = flip(rows % CHUNK)
        pt, ps = pos[:, None], pos[None, :]
        tri.append(np.stack([same & (ps < pt), same & (ps <= pt)]))
        earlier.append(same & ((ps >> SUB_SHIFT) < (pt >> SUB_SHIFT)))
        p64, pl64 = flip(np.arange(CHUNK))[:, None], flip(lane_tok)[None, :]
        cum.append(p64.T <= p64)
        pick.append(np.stack([((p64 - pl64) == dd) & ((p64 >> SUB_SHIFT) == (pl64 >> SUB_SHIFT)) for dd in range(SUB)]))
    f = lambda a: jnp.asarray(np.asarray(a, np.float32))
    sub_head = np.arange(MIX_HEADS * SUB) // SUB
    return {
        "headmask": f(head[:, None] == lane_head[None, :]),
        "headmask_sub": f(sub_head[:, None] == lane_head[None, :]),
        "tri": f(np.stack(tri)),
        "earlier": f(np.stack(earlier)),
        "cum": f(np.stack(cum)).astype(BF16),
        "pick": f(np.stack(pick)),
        "hones": f(lane_head[:, None] == lane_head[None, :]).astype(BF16),
    }


def _round_robin(chains):
    results = [None] * len(chains)
    live = list(range(len(chains)))
    while live:
        for idx in list(live):
            try:
                next(chains[idx])
            except StopIteration as done:
                results[idx] = done.value
                live.remove(idx)
    return results


def _full_spec(a):
    return pl.BlockSpec(a.shape, lambda *_, _n=a.ndim: (0,) * _n)


def _rope(x, cos, sin):
    blocks = [x[:, i:i + 128] for i in range(0, x.shape[-1], 128)]
    fwd = jnp.concatenate([pltpu.roll(b, 32, 1) for b in blocks], axis=1)
    bwd = jnp.concatenate([pltpu.roll(b, 128 - 32, 1) for b in blocks], axis=1)
    first = (_iota(x.shape, 1) & 63) < 32
    return x * cos + jnp.where(first, bwd, fwd) * sin


def _mod_kernel(c_ref, w_ref, b_ref, o_ref):
    o_ref[0] = _mm(_silu(c_ref[...]), w_ref[0]) + b_ref[0]


def _modulation(cc, w_mod, b_mod):
    depth, d, d6 = w_mod.shape
    tn = d6 // 4
    rows = cc.shape[0]
    return pl.pallas_call(
        _mod_kernel,
        grid=(depth, d6 // tn),
        in_specs=[pl.BlockSpec((rows, d), lambda l, j: (0, 0)),
                  pl.BlockSpec((1, d, tn), lambda l, j: (l, 0, j)),
                  pl.BlockSpec((1, 1, tn), lambda l, j: (l, 0, j))],
        out_specs=pl.BlockSpec((1, rows, tn), lambda l, j: (l, 0, j)),
        out_shape=jax.ShapeDtypeStruct((depth, rows, d6), F32),
        compiler_params=_cp("arbitrary", "arbitrary"),
        name="modulation",
    )(cc, w_mod, b_mod.reshape(depth, 1, d6))


def _inproj_kernel(x_ref, mod_ref, w_ref, za_ref, zb_ref, zc_ref, zd_ref):
    d = x_ref.shape[-1]
    mod = mod_ref[0, 0]
    u = (x_ref[0] * (1.0 + mod[:, d:2 * d]) + mod[:, 0:d]).astype(BF16)
    o = 0
    for ref, n in ((za_ref, COLS_A), (zb_ref, COLS_B), (zc_ref, COLS_C), (zd_ref, COLS_D)):
        ref[0] = jnp.dot(u, w_ref[:, o:o + n], preferred_element_type=F32)
        o += n


def _in_projection(x_all, modsel, w_in, tm, n_lat_tiles):
    b, t, d = x_all.shape
    d_in = w_in.shape[-1]
    cols = (COLS_A, COLS_B, COLS_C, COLS_D)
    return pl.pallas_call(
        _inproj_kernel,
        grid=(b, t // tm),
        in_specs=[pl.BlockSpec((1, tm, d), lambda i, j: (i, j, 0)),
                  pl.BlockSpec((1, 1, 1, 6 * d), lambda i, j: (i, (j >= n_lat_tiles).astype(jnp.int32), 0, 0)),
                  pl.BlockSpec((d, d_in), lambda i, j: (0, 0))],
        out_specs=[pl.BlockSpec((1, tm, n), lambda i, j: (i, j, 0)) for n in cols],
        out_shape=[jax.ShapeDtypeStruct((b, t, n), F32) for n in cols],
        compiler_params=_cp("arbitrary", "arbitrary"),
        name="in_projection",
    )(x_all, modsel, w_in)


def _chunk_maps(n_lat_chunks, n_chunks):
    n_ctx_chunks = n_chunks - n_lat_chunks

    def fwd(i):
        return jnp.where(i < n_ctx_chunks, n_lat_chunks + i, i - n_ctx_chunks)

    def bwd(i):
        return jnp.where(i < n_ctx_chunks, n_chunks - 1 - i, n_chunks - 1 - i)

    return fwd, bwd


def _rwkv_direction(dirn, zc, prev_row, next_row, s_ref, p, cst):
    hm = cst["headmask"][...]
    rowi = _iota(zc.shape, 0)
    up = jnp.where(rowi == 0, prev_row, pltpu.roll(zc, 1, 0))
    dn = jnp.where(rowi == CHUNK - 1, next_row, pltpu.roll(zc, CHUNK - 1, 0))
    zs = zc + p["mu"] * (0.5 * (up + dn) - zc)
    r, k, v = zs[:, 0:256], zs[:, 256:512], zs[:, 512:768]
    zw = zs[:, 768 + RWKV_LORA * dirn:768 + RWKV_LORA * (dirn + 1)]
    za = zs[:, 896 + RWKV_LORA * dirn:896 + RWKV_LORA * (dirn + 1)]
    w = p["w0"][dirn:dirn + 1] + _mm(jnp.tanh(zw), p["w2"][dirn])
    lw = -jnp.exp(-_softplus(-w) - 0.5)
    a = _sigmoid(p["a0"][dirn:dirn + 1] + _mm(za, p["a2"][dirn]))
    kkf = k * p["k_k"]
    kk = kkf * lax.rsqrt(_mm_mask_r(kkf * kkf, cst["hones"][...]) + 1e-12)
    kd = k * (1.0 + (a - 1.0) * p["k_a"])
    bv = kk * a

    cum = _mm_mask_l(cst["cum"][dirn], lw)
    total = cum[CHUNK - 1:CHUNK] if dirn == 0 else cum[0:1]
    inv = jnp.exp(-cum)
    tail = jnp.exp(total - cum)
    lhs = jnp.concatenate([_expand(kk * jnp.exp(cum - lw), hm), _expand(r * jnp.exp(cum), hm)], axis=0)
    rhs = jnp.concatenate([_tile(kd * inv), _tile(bv * inv)], axis=0)
    gram = _mm_nt(lhs, rhs)
    yield
    strict, incl = cst["tri"][dirn, 0], cst["tri"][dirn, 1]
    m_k = gram[0:EXP, 0:EXP] * strict
    m_b = gram[0:EXP, EXP:2 * EXP] * strict
    n_k = gram[EXP:2 * EXP, 0:EXP] * incl
    n_b = gram[EXP:2 * EXP, EXP:2 * EXP] * incl

    st = s_ref[...]
    carry = _mm_nt(lhs, st)
    v_e = _expand(v, hm)
    x = carry[0:EXP] + _mm(m_k, v_e)
    yield
    x = x - _mm(m_b, x)
    pw = _mm(m_b, m_b)
    yield
    for step in range(5):
        x = x + _mm(pw, x)
        if step < 4:
            pw = _mm(pw, pw)
        yield
    y_e = carry[EXP:2 * EXP] + _mm(n_k, v_e) - _mm(n_b, x)
    s_ref[...] = st * jnp.exp(total) + _mm_tn(v_e, _expand(kd * tail, hm)) - _mm_tn(x, _expand(bv * tail, hm))
    return _collapse(y_e), r, k, v, zs[:, 1024:1152]


def _rwkv_kernel(n_lat_chunks, n_chunks,
                 zf_ref, zfp_ref, zfn_ref, zb_ref, zbp_ref, zbn_ref,
                 mu_ref, w0_ref, w2_ref, a0_ref, a2_ref, g2_ref, kk_ref, ka_ref, rk_ref,
                 hm_ref, tri_ref, cum_ref, hones_ref,
                 y0_ref, y1_ref, bonus_ref, gate_ref, s_ref):
    i = pl.program_id(1)
    fwd, bwd = _chunk_maps(n_lat_chunks, n_chunks)

    @pl.when(i == 0)
    def _():
        s_ref[...] = jnp.zeros_like(s_ref)

    p = {"mu": mu_ref[...], "w0": w0_ref[...], "w2": w2_ref, "a0": a0_ref[...], "a2": a2_ref,
         "k_k": kk_ref[...], "k_a": ka_ref[...]}
    cst = {"headmask": hm_ref, "tri": tri_ref, "cum": cum_ref, "hones": hones_ref}
    chains = []
    for s in range(REC_BATCH):
        for dirn, (z_ref, zp_ref, zn_ref) in enumerate(((zf_ref, zfp_ref, zfn_ref), (zb_ref, zbp_ref, zbn_ref))):
            c = fwd(i) if dirn == 0 else bwd(i)
            first = jnp.logical_or(c == 0, c == n_lat_chunks)
            last = jnp.logical_or(c == n_lat_chunks - 1, c == n_chunks - 1)
            prev_row = jnp.where(first, 0.0, zp_ref[s][7:8, :])
            next_row = jnp.where(last, 0.0, zn_ref[s][0:1, :])
            chains.append(_rwkv_direction(dirn, z_ref[s], prev_row, next_row, s_ref.at[s, dirn], p, cst))
    for idx, (y, r, k, v, zg) in enumerate(_round_robin(chains)):
        s, dirn = divmod(idx, 2)
        if dirn == 0:
            y0_ref[s] = y
            bonus_ref[s] = _mm_mask_r(r * k * rk_ref[...], hones_ref[...]) * v
            gate_ref[s] = _mm(_sigmoid(zg), g2_ref[...])
        else:
            y1_ref[s] = y


def _rwkv_mixer(z_a, n_lat, prm, cst):
    b, t, _ = z_a.shape
    n_chunks, n_lat_chunks = t // CHUNK, n_lat // CHUNK
    fwd, bwd = _chunk_maps(n_lat_chunks, n_chunks)
    per8 = CHUNK // 8
    last8 = t // 8 - 1
    rb = REC_BATCH

    def zspecs(cm):
        return [pl.BlockSpec((rb, CHUNK, COLS_A), lambda bi, i: (bi, cm(i), 0)),
                pl.BlockSpec((rb, 8, COLS_A), lambda bi, i: (bi, jnp.maximum(cm(i) * per8 - 1, 0), 0)),
                pl.BlockSpec((rb, 8, COLS_A), lambda bi, i: (bi, jnp.minimum((cm(i) + 1) * per8, last8), 0))]

    params = [prm["mu"], prm["w0"], prm["w2"], prm["a0"], prm["a2"], prm["g2"], prm["k_k"], prm["k_a"], prm["r_k"],
              cst["headmask"], cst["tri"], cst["cum"], cst["hones"]]
    yspec_f = pl.BlockSpec((rb, CHUNK, MIX_W), lambda bi, i: (bi, fwd(i), 0))
    yspec_b = pl.BlockSpec((rb, CHUNK, MIX_W), lambda bi, i: (bi, bwd(i), 0))
    shape = jax.ShapeDtypeStruct((b, t, MIX_W), F32)
    return pl.pallas_call(
        functools.partial(_rwkv_kernel, n_lat_chunks, n_chunks),
        grid=(b // rb, n_chunks),
        in_specs=zspecs(fwd) + zspecs(bwd) + [_full_spec(a) for a in params],
        out_specs=[yspec_f, yspec_b, yspec_f, yspec_f],
        out_shape=[shape] * 4,
        scratch_shapes=[pltpu.VMEM((rb, 2, MIX_W, MIX_W), F32)],
        compiler_params=_cp("arbitrary", "arbitrary"),
        name="rwkv7",
    )(z_a, z_a, z_a, z_a, z_a, z_a, *params)


def _ret_direction(dirn, zc, cos, sin, lg, intra, chunk_decay, r_ref, hm):
    q = _rope(zc[:, 0:256], cos, sin)
    k = _rope(zc[:, 256:512], cos, sin) * HEAD_DIM ** -0.5
    v_e = _expand(zc[:, 512:768], hm)
    pos = _iota((CHUNK, MIX_W), 0).astype(F32)
    if dirn == 1:
        pos = (CHUNK - 1.0) - pos
    sc = _mm_nt(_expand(q, hm), _tile(k)) * intra
    rs = r_ref[...]
    o_e = _mm(sc, v_e) + _mm(_expand(q * jnp.exp(lg * (pos + 1.0)), hm), rs)
    kd_e = _expand(k * jnp.exp(lg * ((CHUNK - 1.0) - pos)), hm)
    r_ref[...] = rs * chunk_decay + _mm_tn(kd_e, v_e)
    return _collapse(o_e)


def _ret_kernel(zf_ref, zb_ref, cf_ref, sf_ref, cb_ref, sb_ref, lg_ref, intra_ref, cd_ref, hm_ref,
                y0_ref, y1_ref, r_ref):
    @pl.when(pl.program_id(1) == 0)
    def _():
        r_ref[...] = jnp.zeros_like(r_ref)

    hm = hm_ref[...]
    for s in range(REC_BATCH):
        y0_ref[s] = _ret_direction(0, zf_ref[s], cf_ref[...], sf_ref[...], lg_ref[0:1], intra_ref[0], cd_ref[0],
                                   r_ref.at[s, 0], hm)
        y1_ref[s] = _ret_direction(1, zb_ref[s], cb_ref[...], sb_ref[...], lg_ref[1:2], intra_ref[1], cd_ref[1],
                                   r_ref.at[s, 1], hm)


def _retention_tables(log_gamma, cst):
    lg_rows = jnp.repeat(log_gamma, CHUNK, axis=-1)[:, :, None]
    tok = np.arange(EXP) % CHUNK
    dist = np.abs(tok[:, None] - tok[None, :]).astype(np.float32)
    intra = jnp.exp(lg_rows * dist) * cst["tri"][:, 1]
    same = cst["tri"][:, 1] + jnp.swapaxes(cst["tri"][:, 0], 1, 2)
    return intra, jnp.exp(lg_rows * float(CHUNK)) * same


def _retention_mixer(z_b, n_lat, cos, sin, log_gamma, cst):
    b, t, _ = z_b.shape
    n_chunks, n_lat_chunks = t // CHUNK, n_lat // CHUNK
    fwd, bwd = _chunk_maps(n_lat_chunks, n_chunks)
    shape = jax.ShapeDtypeStruct((b, t, MIX_W), F32)
    rb = REC_BATCH
    intra, chunk_decay = _retention_tables(log_gamma, cst)
    lg_lanes = jnp.repeat(log_gamma, HEAD_DIM, axis=-1)

    def zs(cm):
        return pl.BlockSpec((rb, CHUNK, COLS_B), lambda bi, i: (bi, cm(i), 0))

    def ts(cm):
        return pl.BlockSpec((CHUNK, MIX_W), lambda bi, i: (cm(i), 0))

    consts = [lg_lanes, intra, chunk_decay, cst["headmask"]]
    return pl.pallas_call(
        _ret_kernel,
        grid=(b // rb, n_chunks),
        in_specs=[zs(fwd), zs(bwd), ts(fwd), ts(fwd), ts(bwd), ts(bwd)] + [_full_spec(a) for a in consts],
        out_specs=[pl.BlockSpec((rb, CHUNK, MIX_W), lambda bi, i: (bi, fwd(i), 0)),
                   pl.BlockSpec((rb, CHUNK, MIX_W), lambda bi, i: (bi, bwd(i), 0))],
        out_shape=[shape] * 2,
        scratch_shapes=[pltpu.VMEM((rb, 2, MIX_W, MIX_W), F32)],
        compiler_params=_cp("arbitrary", "arbitrary"),
        name="retention",
    )(z_b, z_b, cos, sin, cos, sin, *consts)


def _hgrn_direction(dirn, zc, lb, log_lb, s_ref, cst):
    hm = cst["headmask"][...]
    qs = _silu(zc[:, 0:256])
    fz = zc[:, 256 + MIX_W * dirn:256 + MIX_W * (dirn + 1)]
    v_e = _expand(zc[:, 768:1024], hm)
    ls_pos, ls_neg = -_softplus(-fz), -_softplus(fz)
    p1, p2 = ls_pos, log_lb + ls_neg
    log_f = jnp.maximum(p1, p2) + jnp.log(1.0 + jnp.exp(-jnp.abs(p1 - p2)))
    kg = (1.0 - lb) * _sigmoid(-fz)
    bc = _mm_mask_l(cst["cum"][dirn], log_f)
    total = bc[CHUNK - 1:CHUNK] if dirn == 0 else bc[0:1]
    excl = bc - log_f

    nsub = CHUNK // SUB
    hm_sub = cst["headmask_sub"][...]
    pieces = []
    for blk in range(nsub):
        first = blk * SUB if dirn == 0 else blk * SUB + SUB - 1
        e_blk = excl[first:first + 1]
        rows = slice(blk * SUB, (blk + 1) * SUB)
        q_s = qs[rows] * jnp.exp(jnp.minimum(bc[rows] - e_blk, 0.0))
        k_s = kg * jnp.exp(jnp.minimum(e_blk - bc, 0.0))
        pieces.append(_mm_nt(_tile(q_s) * hm_sub, _tile(k_s)))
    off = jnp.concatenate([pieces[blk][h * SUB:(h + 1) * SUB] for h in range(MIX_HEADS) for blk in range(nsub)],
                          axis=0)
    prods = []
    for dd in range(SUB):
        sh = dd if dirn == 0 else (CHUNK - dd) % CHUNK
        kr = kg if dd == 0 else pltpu.roll(kg, sh, 0)
        br = bc if dd == 0 else pltpu.roll(bc, sh, 0)
        prods.append((qs * kr * jnp.exp(jnp.minimum(bc - br, 0.0))).astype(BF16))
    val = jnp.dot(jnp.concatenate(prods, axis=0), cst["hones"][...], preferred_element_type=F32)
    near = val[0:CHUNK] * cst["pick"][dirn, 0]
    for dd in range(1, SUB):
        near = near + val[dd * CHUNK:(dd + 1) * CHUNK] * cst["pick"][dirn, dd]
    a_bd = off * cst["earlier"][dirn] + _tile(near) * hm

    st = s_ref[...]
    o_e = _mm(a_bd, v_e) + _mm_nt(_expand(qs * jnp.exp(bc), hm), st)
    s_ref[...] = st * jnp.exp(total) + _mm_tn(v_e, _expand(kg * jnp.exp(total - bc), hm))
    return _collapse(o_e)


def _hgrn_kernel(zf_ref, zb_ref, lb_ref, llb_ref, hm_ref, hms_ref, cum_ref, earlier_ref, pick_ref, hones_ref,
                 y0_ref, y1_ref, s_ref):
    @pl.when(pl.program_id(1) == 0)
    def _():
        s_ref[...] = jnp.zeros_like(s_ref)

    cst = {"headmask": hm_ref, "headmask_sub": hms_ref, "cum": cum_ref, "earlier": earlier_ref, "pick": pick_ref,
           "hones": hones_ref}
    for s in range(REC_BATCH):
        y0_ref[s] = _hgrn_direction(0, zf_ref[s], lb_ref[0:1], llb_ref[0:1], s_ref.at[s, 0], cst)
        y1_ref[s] = _hgrn_direction(1, zb_ref[s], lb_ref[1:2], llb_ref[1:2], s_ref.at[s, 1], cst)


def _hgrn_mixer(z_c, n_lat, lb, log_lb, cst):
    b, t, _ = z_c.shape
    n_chunks, n_lat_chunks = t // CHUNK, n_lat // CHUNK
    fwd, bwd = _chunk_maps(n_lat_chunks, n_chunks)
    shape = jax.ShapeDtypeStruct((b, t, MIX_W), F32)
    rb = REC_BATCH
    consts = [lb, log_lb, cst["headmask"], cst["headmask_sub"], cst["cum"], cst["earlier"], cst["pick"], cst["hones"]]
    return pl.pallas_call(
        _hgrn_kernel,
        grid=(b // rb, n_chunks),
        in_specs=[pl.BlockSpec((rb, CHUNK, COLS_C), lambda bi, i: (bi, fwd(i), 0)),
                  pl.BlockSpec((rb, CHUNK, COLS_C), lambda bi, i: (bi, bwd(i), 0))] + [_full_spec(a) for a in consts],
        out_specs=[pl.BlockSpec((rb, CHUNK, MIX_W), lambda bi, i: (bi, fwd(i), 0)),
                   pl.BlockSpec((rb, CHUNK, MIX_W), lambda bi, i: (bi, bwd(i), 0))],
        out_shape=[shape] * 2,
        scratch_shapes=[pltpu.VMEM((rb, 2, MIX_W, MIX_W), F32)],
        compiler_params=_cp("arbitrary", "arbitrary"),
        name="hgrn2",
    )(z_c, z_c, *consts)


def _attn_kernel(tq, n_lat, zq_ref, zkv_ref, cos_ref, sin_ref, qg_ref, kg_ref, o_ref, k_s, v_s):
    j = pl.program_id(1)
    t = zkv_ref.shape[1]

    @pl.when(j == 0)
    def _():
        kf = zkv_ref[0][:, 256:384]
        kn = kf * lax.rsqrt(_headsum(kf * kf) * (1.0 / HEAD_DIM) + RMS_EPS) * kg_ref[...]
        k_s[...] = _rope(kn, cos_ref[:, 0:128], sin_ref[:, 0:128]).astype(BF16)
        v_s[...] = zkv_ref[0][:, 384:512].astype(BF16)

    row0 = pl.multiple_of(j * tq, tq)
    qf = zq_ref[0][:, 0:256]
    qn = qf * lax.rsqrt(_headsum(qf * qf) * (1.0 / HEAD_DIM) + RMS_EPS) * qg_ref[...]
    q = (_rope(qn, cos_ref[pl.ds(row0, tq), :], sin_ref[pl.ds(row0, tq), :])
         * (HEAD_DIM ** -0.5 * LOG2_E)).astype(BF16)

    def attend(key_lo, key_n):
        outs = []
        for h in range(MIX_HEADS):
            g = h // 2
            sc = lax.dot_general(q[:, h * 64:(h + 1) * 64], k_s[pl.ds(key_lo, key_n), g * 64:(g + 1) * 64],
                                 (((1,), (1,)), ((), ())), preferred_element_type=F32)
            e = jnp.exp2(sc - jnp.max(sc, axis=1, keepdims=True))
            pv = jnp.dot(e.astype(BF16), v_s[pl.ds(key_lo, key_n), g * 64:(g + 1) * 64], preferred_element_type=F32)
            outs.append(pv / jnp.sum(e, axis=1, keepdims=True))
        o_ref[0] = jnp.concatenate(outs, axis=1)

    @pl.when(row0 < n_lat)
    def _():
        attend(0, t)

    @pl.when(row0 >= n_lat)
    def _():
        attend(n_lat, t - n_lat)


def _attention_mixer(z_d, n_lat, tq, nq, cos, sin, q_g, k_g):
    b, t, _ = z_d.shape
    return pl.pallas_call(
        functools.partial(_attn_kernel, tq, n_lat),
        grid=(b, nq),
        in_specs=[pl.BlockSpec((1, tq, COLS_D), lambda bi, j: (bi, j, 0)),
                  pl.BlockSpec((1, t, COLS_D), lambda bi, j: (bi, 0, 0)),
                  pl.BlockSpec((t, MIX_W), lambda bi, j: (0, 0)),
                  pl.BlockSpec((t, MIX_W), lambda bi, j: (0, 0)),
                  pl.BlockSpec((1, MIX_W), lambda bi, j: (0, 0)),
                  pl.BlockSpec((1, 128), lambda bi, j: (0, 0))],
        out_specs=pl.BlockSpec((1, tq, MIX_W), lambda bi, j: (bi, j, 0)),
        out_shape=jax.ShapeDtypeStruct((b, nq * tq, MIX_W), F32),
        scratch_shapes=[pltpu.VMEM((t, 128), BF16), pltpu.VMEM((t, 128), BF16)],
        compiler_params=_cp("arbitrary", "arbitrary"),
        name="attention",
    )(z_d, z_d, cos, sin, q_g, k_g)


def _layer_norm(x, g, b):
    mu = jnp.mean(x, axis=-1, keepdims=True)
    xc = x - mu
    var = jnp.mean(xc * xc, axis=-1, keepdims=True)
    return xc * lax.rsqrt(var + LN_EPS) * g + b


def _group_norm(y, eps):
    mu = _headsum(y) * (1.0 / HEAD_DIM)
    yc = y - mu
    var = _headsum(yc * yc) * (1.0 / HEAD_DIM)
    return yc * lax.rsqrt(var + eps)


def _merge_kernel(alpha, x_ref, mod_ref, ry0, ry1, rbonus, rgate, ty0, ty1, tg, hy0, hy1, hg, at_ref,
                  rln_g, rln_b, tn_g, tn_b, hn_g, wg_ref, wb_ref, wo_ref, ln_g, ln_b, wr_ref,
                  x1_ref, u2_ref, aff_ref):
    d = x_ref.shape[-1]
    mod = mod_ref[0, 0]
    x = x_ref[0]
    u = (x * (1.0 + mod[:, d:2 * d]) + mod[:, 0:d]).astype(BF16)
    hy = hy0[0] + hy1[0]
    branches = (
        (_group_norm(ry0[0] + ry1[0], RWKV_GN_EPS) * rln_g[...] + rln_b[...] + rbonus[0]) * rgate[0],
        (_group_norm(ty0[0] + ty1[0], LN_EPS) * tn_g[...] + tn_b[...]) * _silu(tg[0]),
        hy * lax.rsqrt(_headsum(hy * hy) * (1.0 / HEAD_DIM) + RMS_EPS) * hn_g[...] * _silu(hg[0]),
        at_ref[0],
    )
    merged = None
    for i, br in enumerate(branches):
        term = _sigmoid(jnp.dot(u, wg_ref[i], preferred_element_type=F32)) * _mm(br, wb_ref[i])
        merged = term if merged is None else merged + term
    mix = _mm(merged, wo_ref[...])
    x1 = _layer_norm(alpha * x + mod[:, 2 * d:3 * d] * mix, ln_g[...], ln_b[...])
    x1_ref[0] = x1
    u2 = (x1 * (1.0 + mod[:, 4 * d:5 * d]) + mod[:, 3 * d:4 * d]).astype(BF16)
    u2_ref[0] = u2
    logits = lax.dot_general(wr_ref[...], u2, (((1,), (1,)), ((), ())), preferred_element_type=F32)
    e = jnp.exp(logits - jnp.max(logits, axis=0, keepdims=True))
    aff_ref[0] = e / jnp.sum(e, axis=0, keepdims=True)


def _merge(alpha, x_all, modsel, rw, rt, z_b, hg, z_c, att, prm, tm, n_lat_tiles, n_tiles):
    b, _, d = x_all.shape
    rows = n_tiles * tm

    def tile(w):
        return pl.BlockSpec((1, tm, w), lambda i, j: (i, j, 0))

    def colblock(k):
        return pl.BlockSpec((1, tm, MIX_W), lambda i, j: (i, j, k))

    def full(a):
        return pl.BlockSpec(a.shape, lambda i, j, _n=a.ndim: (0,) * _n)

    params = [prm["rwkv_ln_g"], prm["rwkv_ln_b"], prm["ret_norm_g"], prm["ret_norm_b"], prm["hgrn_norm_g"],
              prm["w_gate"], prm["w_branch"], prm["w_out"], prm["ln1_g"], prm["ln1_b"], prm["w_router_t"]]
    return pl.pallas_call(
        functools.partial(_merge_kernel, alpha),
        grid=(b, n_tiles),
        in_specs=[tile(d), pl.BlockSpec((1, 1, 1, 6 * d),
                                        lambda i, j: (i, (j >= n_lat_tiles).astype(jnp.int32), 0, 0))]
        + [tile(MIX_W)] * 4 + [tile(MIX_W)] * 2 + [colblock(3)] + [tile(MIX_W)] * 2 + [colblock(4)] + [tile(MIX_W)]
        + [full(a) for a in params],
        out_specs=[tile(d), tile(d), pl.BlockSpec((1, N_EXPERTS, tm), lambda i, j: (i, 0, j))],
        out_shape=[jax.ShapeDtypeStruct((b, rows, d), F32), jax.ShapeDtypeStruct((b, rows, d), BF16),
                   jax.ShapeDtypeStruct((b, N_EXPERTS, rows), F32)],
        compiler_params=_cp("arbitrary", "arbitrary"),
        name="merge",
    )(x_all, modsel, *rw, rt[0], rt[1], z_b, hg[0], hg[1], z_c, att, *params)


def _route_kernel(start, n, cap, aff_ref, code_ref, gate_ref):
    a = aff_ref[0][:, start:start + n]
    bits = pltpu.bitcast(a, jnp.int32)

    def bisect(_, lohi):
        lo, hi = lohi
        mid = lo + ((hi - lo + 1) >> 1)
        ok = jnp.sum(jnp.where(bits >= mid, 1.0, 0.0), axis=1, keepdims=True) >= cap
        return jnp.where(ok, mid, lo), jnp.where(ok, hi, mid - 1)

    lo0 = jnp.zeros((N_EXPERTS, 1), jnp.int32)
    thr, _ = lax.fori_loop(0, 31, bisect, (lo0, jnp.full((N_EXPERTS, 1), 0x7F800000, jnp.int32)))
    gt, eq = bits > thr, bits == thr
    need = cap - jnp.sum(jnp.where(gt, 1.0, 0.0), axis=1, keepdims=True)
    blk = min(n, ROUTE_BLOCK)
    upper = jnp.where(_iota((blk, blk), 0) <= _iota((blk, blk), 1), 1.0, 0.0).astype(BF16)

    def prefix_count(m):
        parts, running = [], jnp.zeros((N_EXPERTS, 1), F32)
        for o in range(0, n, blk):
            pre = jnp.dot(jnp.where(m[:, o:o + blk], 1.0, 0.0).astype(BF16), upper, preferred_element_type=F32)
            parts.append(pre + running)
            running = running + pre[:, blk - 1:blk]
        return jnp.concatenate(parts, axis=1)

    sel = gt | (eq & (prefix_count(eq) <= need))
    rank = prefix_count(sel) - 1.0
    code_ref[0] = jnp.where(sel, rank.astype(jnp.int32), -1)
    gate_ref[0] = a


def _route(aff, start, n, cap):
    b, e, rows = aff.shape
    code, gate = pl.pallas_call(
        functools.partial(_route_kernel, start, n, cap),
        grid=(b,),
        in_specs=[pl.BlockSpec((1, e, rows), lambda i: (i, 0, 0))],
        out_specs=[pl.BlockSpec((1, e, n), lambda i: (i, 0, 0))] * 2,
        out_shape=[jax.ShapeDtypeStruct((b, e, n), jnp.int32), jax.ShapeDtypeStruct((b, e, n), F32)],
        compiler_params=_cp("arbitrary"),
        name="route",
    )(aff)
    return code.reshape(b, e, 1, n), gate.reshape(b, e, 1, n)


def _moe_kernel(bg, cap_pad, code_ref, gate_ref, u_ref, w1_ref, w3_ref, w2_ref, o_ref):
    @pl.when(pl.program_id(1) == 0)
    def _():
        o_ref[...] = jnp.zeros_like(o_ref)

    n = u_ref.shape[1]
    slot = _iota((cap_pad, n), 0)
    hits = [slot == code_ref[s, 0] for s in range(bg)]
    xs = jnp.concatenate([jnp.dot(jnp.where(h, 1.0, 0.0).astype(BF16), u_ref[s], preferred_element_type=F32)
                          for s, h in enumerate(hits)], axis=0).astype(BF16)
    hmid = _silu(jnp.dot(xs, w1_ref[0], preferred_element_type=F32)) * jnp.dot(xs, w3_ref[0],
                                                                               preferred_element_type=F32)
    y = jnp.dot(hmid.astype(BF16), w2_ref[0], preferred_element_type=F32)
    for s, h in enumerate(hits):
        weights = jnp.where(h, gate_ref[s, 0], 0.0)
        o_ref[s] += _mm_tn(weights, y[s * cap_pad:(s + 1) * cap_pad])


def _moe(code, gate, u2, row_block, n, cap, bg, w1, w3, w2):
    b = u2.shape[0]
    d = u2.shape[-1]
    cap_pad = max(cap, 128)
    f = w1.shape[-1]
    return pl.pallas_call(
        functools.partial(_moe_kernel, bg, cap_pad),
        grid=(b // bg, N_EXPERTS),
        in_specs=[pl.BlockSpec((bg, 1, 1, n), lambda i, e: (i, e, 0, 0)),
                  pl.BlockSpec((bg, 1, 1, n), lambda i, e: (i, e, 0, 0)),
                  pl.BlockSpec((bg, n, d), lambda i, e: (i, row_block, 0)),
                  pl.BlockSpec((1, d, f), lambda i, e: (e, 0, 0)),
                  pl.BlockSpec((1, d, f), lambda i, e: (e, 0, 0)),
                  pl.BlockSpec((1, f, d), lambda i, e: (e, 0, 0))],
        out_specs=pl.BlockSpec((bg, n, d), lambda i, e: (i, 0, 0)),
        out_shape=jax.ShapeDtypeStruct((b, n, d), F32),
        compiler_params=_cp("arbitrary", "arbitrary"),
        name="expert_ffn",
    )(code, gate, u2, w1, w3, w2)


def _ln2_kernel(alpha, x_ref, f_ref, mod_ref, g_ref, b_ref, o_ref):
    d = x_ref.shape[-1]
    o_ref[0] = _layer_norm(alpha * x_ref[0] + mod_ref[0, 0][:, 5 * d:6 * d] * f_ref[0], g_ref[...], b_ref[...])


def _ln2(alpha, x1, x1_tile0, ffn, modsel, seg, g, bias, tm):
    b, rows, d = ffn.shape
    return pl.pallas_call(
        functools.partial(_ln2_kernel, alpha),
        grid=(b, rows // tm),
        in_specs=[pl.BlockSpec((1, tm, d), lambda i, j: (i, j + x1_tile0, 0)),
                  pl.BlockSpec((1, tm, d), lambda i, j: (i, j, 0)),
                  pl.BlockSpec((1, 1, 1, 6 * d), lambda i, j: (i, seg, 0, 0)),
                  pl.BlockSpec((1, d), lambda i, j: (0, 0)),
                  pl.BlockSpec((1, d), lambda i, j: (0, 0))],
        out_specs=pl.BlockSpec((1, tm, d), lambda i, j: (i, j, 0)),
        out_shape=jax.ShapeDtypeStruct((b, rows, d), F32),
        compiler_params=_cp("arbitrary", "arbitrary"),
        name="ln2",
    )(x1, ffn, modsel, g, bias)


def _rope_tables(n_lat, n_ctx):
    rows = n_lat // GRID_W
    row = jnp.repeat(jnp.arange(rows), GRID_W)
    col = jnp.tile(jnp.arange(GRID_W), rows)
    n_freq = HEAD_DIM // 4
    inv = ROPE_BASE ** (-jnp.arange(n_freq, dtype=F32) / n_freq)
    ang = jnp.concatenate([row[:, None] * inv, col[:, None] * inv], axis=-1)
    cos, sin = jnp.cos(ang), jnp.sin(ang)
    cos64 = jnp.concatenate([cos, cos], axis=-1)
    sin64 = jnp.concatenate([-sin, sin], axis=-1)
    cos64 = jnp.concatenate([cos64, jnp.ones((n_ctx, HEAD_DIM), F32)], axis=0)
    sin64 = jnp.concatenate([sin64, jnp.zeros((n_ctx, HEAD_DIM), F32)], axis=0)
    return jnp.tile(cos64, (1, MIX_HEADS)), jnp.tile(sin64, (1, MIX_HEADS))


def kernel(x, c, ctx, c_ctx, w_mod, b_mod, w_in, rwkv_mu, rwkv_w0, rwkv_w2, rwkv_a0, rwkv_a2, rwkv_g2, rwkv_kk, rwkv_ka, rwkv_rk, rwkv_ln_g, rwkv_ln_b, ret_decay, ret_norm_g, ret_norm_b, hgrn_lb, hgrn_norm_g, attn_q_g, attn_k_g, w_gate, w_branch, w_out, ln1_g, ln1_b, w_router, w_e1, w_e3, w_e2, ln2_g, ln2_b):
    bsz, n_lat, d = x.shape
    n_ctx = ctx.shape[1]
    depth = w_mod.shape[0]
    t = n_lat + n_ctx
    tm = min(256, n_ctx)
    assert n_lat % tm == 0 and n_ctx % tm == 0 and tm % CHUNK == 0 and n_lat % n_ctx == 0 and bsz % REC_BATCH == 0
    n_lat_tiles, n_tiles = n_lat // tm, t // tm
    alpha = (2 * depth) ** 0.25

    cos, sin = _rope_tables(n_lat, n_ctx)
    lb_w = jax.nn.softmax(hgrn_lb.astype(F32), axis=1)
    lower = jnp.cumsum(lb_w, axis=1) - lb_w[:, :1]
    log_lower = jnp.log(jnp.maximum(lower, HGRN_LB_FLOOR))
    log_gamma = jax.nn.log_sigmoid(ret_decay.astype(F32))
    cst = _scan_constants()

    rows = 8 * ((bsz + 1 + 7) // 8)
    cc = jnp.zeros((rows, d), F32).at[:bsz].set(c).at[bsz].set(c_ctx)
    mods = _modulation(cc, w_mod.astype(BF16), b_mod)

    x_all = jnp.concatenate([x, ctx], axis=1)
    row = lambda a: a.reshape(1, -1)
    for l in range(depth):
        need_ctx = l < depth - 1
        modsel = jnp.stack([mods[l, :bsz], jnp.broadcast_to(mods[l, bsz], (bsz, 6 * d))],
                           axis=1).reshape(bsz, 2, 1, 6 * d)
        z_a, z_b, z_c, z_d = _in_projection(x_all, modsel, w_in[l].astype(BF16), tm, n_lat_tiles)
        rw = _rwkv_mixer(z_a, n_lat, {
            "mu": row(rwkv_mu[l]), "w0": rwkv_w0[l], "w2": rwkv_w2[l].astype(BF16), "a0": rwkv_a0[l],
            "a2": rwkv_a2[l].astype(BF16), "g2": rwkv_g2[l].astype(BF16), "k_k": row(rwkv_kk[l]),
            "k_a": row(rwkv_ka[l]), "r_k": row(rwkv_rk[l])}, cst)
        rt = _retention_mixer(z_b, n_lat, cos, sin, log_gamma[l], cst)
        hg = _hgrn_mixer(z_c, n_lat, lower[:, l], log_lower[:, l], cst)
        n_out_tiles = n_tiles if need_ctx else n_lat_tiles
        att = _attention_mixer(z_d, n_lat, tm, n_out_tiles, cos, sin,
                               row(jnp.tile(attn_q_g[l], MIX_HEADS)), row(jnp.tile(attn_k_g[l], 2)))
        x1, u2, aff = _merge(alpha, x_all, modsel, rw, rt, z_b, hg, z_c, att, {
            "rwkv_ln_g": row(rwkv_ln_g[l]), "rwkv_ln_b": row(rwkv_ln_b[l]), "ret_norm_g": row(ret_norm_g[l]),
            "ret_norm_b": row(ret_norm_b[l]), "hgrn_norm_g": row(hgrn_norm_g[l]),
            "w_gate": w_gate[l].astype(BF16), "w_branch": w_branch[l].astype(BF16), "w_out": w_out[l].astype(BF16),
            "ln1_g": row(ln1_g[l]), "ln1_b": row(ln1_b[l]), "w_router_t": w_router[l].T.astype(BF16)},
            tm, n_lat_tiles, n_out_tiles)
        w1, w3, w2 = w_e1[l].astype(BF16), w_e3[l].astype(BF16), w_e2[l].astype(BF16)
        cap_lat = EC_CAPACITY * n_lat // N_EXPERTS
        code, gate = _route(aff, 0, n_lat, cap_lat)
        ffn = _moe(code, gate, u2, 0, n_lat, cap_lat, 1, w1, w3, w2)
        x_lat = _ln2(alpha, x1, 0, ffn, modsel, 0, row(ln2_g[l]), row(ln2_b[l]), tm)
        if not need_ctx:
            return x_lat
        cap_ctx = EC_CAPACITY * n_ctx // N_EXPERTS
        code_c, gate_c = _route(aff, n_lat, n_ctx, cap_ctx)
        ffn_c = _moe(code_c, gate_c, u2, n_lat // n_ctx, n_ctx, cap_ctx, 8 if bsz % 8 == 0 else 1, w1, w3, w2)
        x_ctx = _ln2(alpha, x1, n_lat_tiles, ffn_c, modsel, 1, row(ln2_g[l]), row(ln2_b[l]), tm)
        x_all = jnp.concatenate([x_lat, x_ctx], axis=1)
    return x_all[:, :n_lat]
```

```python
import functools

import jax
import jax.numpy as jnp
import numpy as np
from jax import lax
from jax.experimental import pallas as pl
from jax.experimental.pallas import tpu as pltpu

F32 = jnp.float32
BF16 = jnp.bfloat16

HEAD_DIM = 64
MIX_HEADS = 4
MIX_W = MIX_HEADS * HEAD_DIM
GRID_W = 64
ROPE_BASE = 10000.0
N_EXPERTS = 16
EC_CAPACITY = 2
LN_EPS = 1e-5
RMS_EPS = 1e-6
RWKV_GN_EPS = HEAD_DIM * 1e-5
HGRN_LB_FLOOR = 1e-20
LOG2_E = 1.4426950408889634
RWKV_LORA = 64
COLS_A, COLS_B, COLS_C, COLS_D = 1152, 1024, 1280, 512
CHUNK = 64
SUB_SHIFT = 4
SUB = 1 << SUB_SHIFT
EXP = MIX_HEADS * CHUNK
REC_BATCH = 4
ROUTE_BLOCK = 256
VMEM_LIMIT = 56 * 1024 * 1024


def _cp(*sem):
    return pltpu.CompilerParams(dimension_semantics=sem, vmem_limit_bytes=VMEM_LIMIT)


def _mm(a, b):
    return jnp.dot(a.astype(BF16), b.astype(BF16), preferred_element_type=F32)


def _mm_nt(a, b):
    return lax.dot_general(a.astype(BF16), b.astype(BF16), (((1,), (1,)), ((), ())), preferred_element_type=F32)


def _mm_tn(a, b):
    return lax.dot_general(a.astype(BF16), b.astype(BF16), (((0,), (0,)), ((), ())), preferred_element_type=F32)


def _split(x):
    hi = x.astype(BF16)
    lo = (x - hi.astype(F32)).astype(BF16)
    return hi, lo


def _mm_mask_l(mask_bf16, x):
    hi, lo = _split(x)
    return (jnp.dot(mask_bf16, hi, preferred_element_type=F32) + jnp.dot(mask_bf16, lo, preferred_element_type=F32))


def _mm_mask_r(x, mask_bf16):
    hi, lo = _split(x)
    return (jnp.dot(hi, mask_bf16, preferred_element_type=F32) + jnp.dot(lo, mask_bf16, preferred_element_type=F32))


def _iota(shape, dim):
    return lax.broadcasted_iota(jnp.int32, shape, dim)


def _headsum(x, hones):
    n = x.shape[-1]
    return _mm_mask_r(x, hones[0:n, 0:n])


def _sigmoid(x):
    return 1.0 / (1.0 + jnp.exp(-x))


def _silu(x):
    return x * _sigmoid(x)


def _softplus(x):
    return jnp.maximum(x, 0.0) + jnp.log(1.0 + jnp.exp(-jnp.abs(x)))


def _expand(x, headmask):
    tiled = jnp.concatenate([x.astype(BF16)] * MIX_HEADS, axis=0)
    return pltpu.bitcast(pltpu.bitcast(tiled, jnp.uint32) & headmask, BF16)


def _scan_constants():
    rows = np.arange(EXP)
    head = rows // CHUNK
    same = head[:, None] == head[None, :]
    lane_head = np.arange(MIX_W) // HEAD_DIM
    lane_tok = np.arange(MIX_W) % HEAD_DIM
    tri, cum, earlier, pick = [], [], [], []
    for dirn in (0, 1):
        flip = (lambda a: a) if dirn == 0 else (lambda a: CHUNK - 1 - a)
        p64, pl64 = flip(np.arange(CHUNK))[:, None], flip(lane_tok)[None, :]
        tri.append(np.stack([pl64 < p64, pl64 <= p64]))
        earlier.append((pl64 >> SUB_SHIFT) < (p64 >> SUB_SHIFT))
        cum.append(p64.T <= p64)
        pick.append(np.stack([((p64 - pl64) == dd) & ((p64 >> SUB_SHIFT) == (pl64 >> SUB_SHIFT)) for dd in range(SUB)]))
    f = lambda a: jnp.asarray(np.asarray(a, np.float32))
    return {
        "headmask": jnp.asarray(np.where(head[::2, None] == lane_head[None, :], 0xFFFFFFFF, 0).astype(np.uint32)),
        "same": f(same),
        "tri": f(np.stack(tri)),
        "earlier": f(np.stack(earlier)),
        "cum": f(np.stack(cum)).astype(BF16),
        "pick": f(np.stack(pick)),
        "hones": f(lane_head[:, None] == lane_head[None, :]).astype(BF16),
    }


def _round_robin(chains):
    results = [None] * len(chains)
    live = list(range(len(chains)))
    while live:
        for idx in list(live):
            try:
                next(chains[idx])
            except StopIteration as done:
                results[idx] = done.value
                live.remove(idx)
    return results


def _full_spec(a):
    return pl.BlockSpec(a.shape, lambda *_, _n=a.ndim: (0,) * _n)


def _rope(x, cos, sin):
    blocks = [x[:, i:i + 128] for i in range(0, x.shape[-1], 128)]
    fwd = jnp.concatenate([pltpu.roll(b, 32, 1) for b in blocks], axis=1)
    bwd = jnp.concatenate([pltpu.roll(b, 128 - 32, 1) for b in blocks], axis=1)
    first = (_iota(x.shape, 1) & 63) < 32
    return x * cos + jnp.where(first, bwd, fwd) * sin


def _mod_kernel(c_ref, w_ref, b_ref, o_ref):
    o_ref[0] = _mm(_silu(c_ref[...]), w_ref[0]) + b_ref[0]


def _modulation(cc, w_mod, b_mod):
    depth, d, d6 = w_mod.shape
    tn = d6 // 4
    rows = cc.shape[0]
    return pl.pallas_call(
        _mod_kernel,
        grid=(depth, d6 // tn),
        in_specs=[pl.BlockSpec((rows, d), lambda l, j: (0, 0)),
                  pl.BlockSpec((1, d, tn), lambda l, j: (l, 0, j)),
                  pl.BlockSpec((1, 1, tn), lambda l, j: (l, 0, j))],
        out_specs=pl.BlockSpec((1, rows, tn), lambda l, j: (l, 0, j)),
        out_shape=jax.ShapeDtypeStruct((depth, rows, d6), F32),
        compiler_params=_cp("arbitrary", "arbitrary"),
        name="modulation",
    )(cc, w_mod, b_mod.reshape(depth, 1, d6))


def _inproj_kernel(n_lat_tiles, xl_ref, xc_ref, mod_ref, w_ref, za_ref, zb_ref, zc_ref, zd_ref):
    d = xl_ref.shape[-1]
    mod = mod_ref[0, 0]
    x = jnp.where(pl.program_id(1) < n_lat_tiles, xl_ref[0], xc_ref[0])
    u = (x * (1.0 + mod[:, d:2 * d]) + mod[:, 0:d]).astype(BF16)
    o = 0
    for ref, n in ((za_ref, COLS_A), (zb_ref, COLS_B), (zc_ref, COLS_C), (zd_ref, COLS_D)):
        ref[0] = jnp.dot(u, w_ref[:, o:o + n], preferred_element_type=F32)
        o += n


def _two_stream_specs(tm, d, n_lat_tiles):
    return [pl.BlockSpec((1, tm, d), lambda i, j: (i, jnp.minimum(j, n_lat_tiles - 1), 0)),
            pl.BlockSpec((1, tm, d), lambda i, j: (i, jnp.maximum(j - n_lat_tiles, 0), 0))]


def _in_projection(x_lat, x_ctx, modsel, w_in, tm, n_lat_tiles):
    b, n_lat, d = x_lat.shape
    t = n_lat + x_ctx.shape[1]
    d_in = w_in.shape[-1]
    cols = (COLS_A, COLS_B, COLS_C, COLS_D)
    return pl.pallas_call(
        functools.partial(_inproj_kernel, n_lat_tiles),
        grid=(b, t // tm),
        in_specs=_two_stream_specs(tm, d, n_lat_tiles) + [
            pl.BlockSpec((1, 1, 1, 6 * d), lambda i, j: (i, (j >= n_lat_tiles).astype(jnp.int32), 0, 0)),
            pl.BlockSpec((d, d_in), lambda i, j: (0, 0))],
        out_specs=[pl.BlockSpec((1, tm, n), lambda i, j: (i, j, 0)) for n in cols],
        out_shape=[jax.ShapeDtypeStruct((b, t, n), F32) for n in cols],
        compiler_params=_cp("arbitrary", "arbitrary"),
        name="in_projection",
    )(x_lat, x_ctx, modsel, w_in)


def _chunk_maps(n_lat_chunks, n_chunks):
    n_ctx_chunks = n_chunks - n_lat_chunks

    def fwd(i):
        return jnp.where(i < n_ctx_chunks, n_lat_chunks + i, i - n_ctx_chunks)

    def bwd(i):
        return jnp.where(i < n_ctx_chunks, n_chunks - 1 - i, n_chunks - 1 - i)

    return fwd, bwd


def _rwkv_direction(dirn, zc, prev_row, next_row, s_ref, p, cst):
    hm = cst["headmask"][...]
    rowi = _iota(zc.shape, 0)
    up = jnp.where(rowi == 0, prev_row, pltpu.roll(zc, 1, 0))
    dn = jnp.where(rowi == CHUNK - 1, next_row, pltpu.roll(zc, CHUNK - 1, 0))
    zs = zc + p["mu"] * (0.5 * (up + dn) - zc)
    r, k, v = zs[:, 0:256], zs[:, 256:512], zs[:, 512:768]
    zw = zs[:, 768 + RWKV_LORA * dirn:768 + RWKV_LORA * (dirn + 1)]
    za = zs[:, 896 + RWKV_LORA * dirn:896 + RWKV_LORA * (dirn + 1)]
    w = p["w0"][dirn:dirn + 1] + _mm(jnp.tanh(zw), p["w2"][dirn])
    lw = -jnp.exp(-_softplus(-w) - 0.5)
    a = _sigmoid(p["a0"][dirn:dirn + 1] + _mm(za, p["a2"][dirn]))
    kkf = k * p["k_k"]
    kk = kkf * lax.rsqrt(_headsum(kkf * kkf, cst["hones"][...]) + 1e-12)
    kd = k * (1.0 + (a - 1.0) * p["k_a"])
    bv = kk * a

    cum = _mm_mask_l(cst["cum"][dirn], lw)
    total = cum[CHUNK - 1:CHUNK] if dirn == 0 else cum[0:1]
    inv = jnp.exp(-cum)
    tail = jnp.exp(total - cum)
    lhs = jnp.concatenate([kk * jnp.exp(cum - lw), r * jnp.exp(cum)], axis=0)
    g_k = _mm_nt(lhs, _expand(kd * inv, hm))
    g_b = _mm_nt(lhs, _expand(bv * inv, hm))
    yield
    strict, incl = cst["tri"][dirn, 0], cst["tri"][dirn, 1]
    m_b = g_b[0:CHUNK] * strict
    n_b = g_b[CHUNK:2 * CHUNK] * incl
    mn_k = g_k * jnp.concatenate([strict, incl], axis=0)

    st = s_ref[...]
    carry = _mm_nt(lhs, st) + _mm(mn_k, _expand(v, hm))
    x = carry[0:CHUNK]
    yield
    x = x - _mm(m_b, _expand(x, hm))
    pw = _mm(m_b, _expand(m_b, hm))
    yield
    for step in range(5):
        x = x + _mm(pw, _expand(x, hm))
        if step < 4:
            pw = _mm(pw, _expand(pw, hm))
        yield
    y = carry[CHUNK:2 * CHUNK] - _mm(n_b, _expand(x, hm))
    upd = _mm_tn(jnp.concatenate([v, x], axis=0), jnp.concatenate([kd * tail, -(bv * tail)], axis=0))
    s_ref[...] = st * jnp.exp(total) + upd * cst["same"][...]
    return y, r, k, v, zs[:, 1024:1152]


def _rwkv_kernel(n_lat_chunks, n_chunks,
                 zf_ref, zfp_ref, zfn_ref, zb_ref, zbp_ref, zbn_ref,
                 mu_ref, w0_ref, w2_ref, a0_ref, a2_ref, g2_ref, kk_ref, ka_ref, rk_ref,
                 hm_ref, same_ref, tri_ref, cum_ref, hones_ref,
                 y0_ref, y1_ref, bonus_ref, gate_ref, s_ref):
    i = pl.program_id(1)
    fwd, bwd = _chunk_maps(n_lat_chunks, n_chunks)

    @pl.when(i == 0)
    def _():
        s_ref[...] = jnp.zeros_like(s_ref)

    p = {"mu": mu_ref[...], "w0": w0_ref[...], "w2": w2_ref, "a0": a0_ref[...], "a2": a2_ref,
         "k_k": kk_ref[...], "k_a": ka_ref[...]}
    cst = {"headmask": hm_ref, "same": same_ref, "tri": tri_ref, "cum": cum_ref, "hones": hones_ref}
    chains = []
    for s in range(REC_BATCH):
        for dirn, (z_ref, zp_ref, zn_ref) in enumerate(((zf_ref, zfp_ref, zfn_ref), (zb_ref, zbp_ref, zbn_ref))):
            c = fwd(i) if dirn == 0 else bwd(i)
            first = jnp.logical_or(c == 0, c == n_lat_chunks)
            last = jnp.logical_or(c == n_lat_chunks - 1, c == n_chunks - 1)
            prev_row = jnp.where(first, 0.0, zp_ref[s][7:8, :])
            next_row = jnp.where(last, 0.0, zn_ref[s][0:1, :])
            chains.append(_rwkv_direction(dirn, z_ref[s], prev_row, next_row, s_ref.at[s, dirn], p, cst))
    for idx, (y, r, k, v, zg) in enumerate(_round_robin(chains)):
        s, dirn = divmod(idx, 2)
        if dirn == 0:
            y0_ref[s] = y
            bonus_ref[s] = _headsum(r * k * rk_ref[...], hones_ref[...]) * v
            gate_ref[s] = _mm(_sigmoid(zg), g2_ref[...])
        else:
            y1_ref[s] = y


def _rwkv_mixer(z_a, n_lat, prm, cst):
    b, t, _ = z_a.shape
    n_chunks, n_lat_chunks = t // CHUNK, n_lat // CHUNK
    fwd, bwd = _chunk_maps(n_lat_chunks, n_chunks)
    per8 = CHUNK // 8
    last8 = t // 8 - 1
    rb = REC_BATCH

    def zspecs(cm):
        return [pl.BlockSpec((rb, CHUNK, COLS_A), lambda bi, i: (bi, cm(i), 0)),
                pl.BlockSpec((rb, 8, COLS_A), lambda bi, i: (bi, jnp.maximum(cm(i) * per8 - 1, 0), 0)),
                pl.BlockSpec((rb, 8, COLS_A), lambda bi, i: (bi, jnp.minimum((cm(i) + 1) * per8, last8), 0))]

    params = [prm["mu"], prm["w0"], prm["w2"], prm["a0"], prm["a2"], prm["g2"], prm["k_k"], prm["k_a"], prm["r_k"],
              cst["headmask"], cst["same"], cst["tri"], cst["cum"], cst["hones"]]
    yspec_f = pl.BlockSpec((rb, CHUNK, MIX_W), lambda bi, i: (bi, fwd(i), 0))
    yspec_b = pl.BlockSpec((rb, CHUNK, MIX_W), lambda bi, i: (bi, bwd(i), 0))
    shape = jax.ShapeDtypeStruct((b, t, MIX_W), F32)
    return pl.pallas_call(
        functools.partial(_rwkv_kernel, n_lat_chunks, n_chunks),
        grid=(b // rb, n_chunks),
        in_specs=zspecs(fwd) + zspecs(bwd) + [_full_spec(a) for a in params],
        out_specs=[yspec_f, yspec_b, yspec_f, yspec_f],
        out_shape=[shape] * 4,
        scratch_shapes=[pltpu.VMEM((rb, 2, MIX_W, MIX_W), F32)],
        compiler_params=_cp("arbitrary", "arbitrary"),
        name="rwkv7",
    )(z_a, z_a, z_a, z_a, z_a, z_a, *params)


def _ret_direction(dirn, zc, cos, sin, lg, intra, chunk_decay, r_ref, hm, same):
    q = _rope(zc[:, 0:256], cos, sin)
    k = _rope(zc[:, 256:512], cos, sin) * HEAD_DIM ** -0.5
    v = zc[:, 512:768]
    pos = _iota((CHUNK, MIX_W), 0).astype(F32)
    if dirn == 1:
        pos = (CHUNK - 1.0) - pos
    sc = _mm_nt(q, _expand(k, hm)) * intra
    rs = r_ref[...]
    o = _mm(sc, _expand(v, hm)) + _mm(q * jnp.exp(lg * (pos + 1.0)), rs)
    r_ref[...] = rs * chunk_decay + _mm_tn(k * jnp.exp(lg * ((CHUNK - 1.0) - pos)), v) * same
    return o


def _ret_kernel(zf_ref, zb_ref, cf_ref, sf_ref, cb_ref, sb_ref, lg_ref, intra_ref, cd_ref, hm_ref, same_ref,
                y0_ref, y1_ref, r_ref):
    @pl.when(pl.program_id(1) == 0)
    def _():
        r_ref[...] = jnp.zeros_like(r_ref)

    hm, same = hm_ref[...], same_ref[...]
    for s in range(REC_BATCH):
        y0_ref[s] = _ret_direction(0, zf_ref[s], cf_ref[...], sf_ref[...], lg_ref[0:1], intra_ref[0], cd_ref[0],
                                   r_ref.at[s, 0], hm, same)
        y1_ref[s] = _ret_direction(1, zb_ref[s], cb_ref[...], sb_ref[...], lg_ref[1:2], intra_ref[1], cd_ref[1],
                                   r_ref.at[s, 1], hm, same)


def _retention_tables(log_gamma, cst):
    lane_tok = np.arange(MIX_W) % HEAD_DIM
    dist = np.abs(np.arange(CHUNK)[:, None] - lane_tok[None, :]).astype(np.float32)
    lg_lanes = jnp.repeat(log_gamma, HEAD_DIM, axis=-1)
    intra = jnp.exp(lg_lanes[:, None, :] * dist) * cst["tri"][:, 1]
    return intra, jnp.exp(lg_lanes[:, :, None] * float(CHUNK)) * cst["same"]


def _retention_mixer(z_b, n_lat, cos, sin, log_gamma, cst):
    b, t, _ = z_b.shape
    n_chunks, n_lat_chunks = t // CHUNK, n_lat // CHUNK
    fwd, bwd = _chunk_maps(n_lat_chunks, n_chunks)
    shape = jax.ShapeDtypeStruct((b, t, MIX_W), F32)
    rb = REC_BATCH
    intra, chunk_decay = _retention_tables(log_gamma, cst)
    lg_lanes = jnp.repeat(log_gamma, HEAD_DIM, axis=-1)

    def zs(cm):
        return pl.BlockSpec((rb, CHUNK, COLS_B), lambda bi, i: (bi, cm(i), 0))

    def ts(cm):
        return pl.BlockSpec((CHUNK, MIX_W), lambda bi, i: (cm(i), 0))

    consts = [lg_lanes, intra, chunk_decay, cst["headmask"], cst["same"]]
    return pl.pallas_call(
        _ret_kernel,
        grid=(b // rb, n_chunks),
        in_specs=[zs(fwd), zs(bwd), ts(fwd), ts(fwd), ts(bwd), ts(bwd)] + [_full_spec(a) for a in consts],
        out_specs=[pl.BlockSpec((rb, CHUNK, MIX_W), lambda bi, i: (bi, fwd(i), 0)),
                   pl.BlockSpec((rb, CHUNK, MIX_W), lambda bi, i: (bi, bwd(i), 0))],
        out_shape=[shape] * 2,
        scratch_shapes=[pltpu.VMEM((rb, 2, MIX_W, MIX_W), F32)],
        compiler_params=_cp("arbitrary", "arbitrary"),
        name="retention",
    )(z_b, z_b, cos, sin, cos, sin, *consts)


def _hgrn_direction(dirn, zc, lb, log_lb, s_ref, cst):
    hm = cst["headmask"][...]
    qs = _silu(zc[:, 0:256])
    fz = zc[:, 256 + MIX_W * dirn:256 + MIX_W * (dirn + 1)]
    v = zc[:, 768:1024]
    tail_term = jnp.log(1.0 + jnp.exp(-jnp.abs(fz)))
    ls_pos = -(jnp.maximum(-fz, 0.0) + tail_term)
    ls_neg = -(jnp.maximum(fz, 0.0) + tail_term)
    p1, p2 = ls_pos, log_lb + ls_neg
    log_f = jnp.maximum(p1, p2) + jnp.log(1.0 + jnp.exp(-jnp.abs(p1 - p2)))
    kg = (1.0 - lb) * _sigmoid(-fz)
    bc = _mm_mask_l(cst["cum"][dirn], log_f * LOG2_E)
    total = bc[CHUNK - 1:CHUNK] if dirn == 0 else bc[0:1]
    excl = bc - log_f * LOG2_E

    nsub = CHUNK // SUB
    pieces = []
    for blk in range(nsub):
        first = blk * SUB if dirn == 0 else blk * SUB + SUB - 1
        e_blk = excl[first:first + 1]
        rows = slice(blk * SUB, (blk + 1) * SUB)
        q_s = qs[rows] * jnp.exp2(jnp.minimum(bc[rows] - e_blk, 0.0))
        k_s = kg * jnp.exp2(jnp.minimum(e_blk - bc, 0.0))
        pieces.append(_mm_nt(q_s, _expand(k_s, hm)))
    off = jnp.concatenate(pieces, axis=0)
    prods = []
    for dd in range(SUB):
        sh = dd if dirn == 0 else (CHUNK - dd) % CHUNK
        kr = kg if dd == 0 else pltpu.roll(kg, sh, 0)
        br = bc if dd == 0 else pltpu.roll(bc, sh, 0)
        prods.append((qs * kr * jnp.exp2(jnp.minimum(bc - br, 0.0))).astype(BF16))
    val = jnp.dot(jnp.concatenate(prods, axis=0), cst["hones"][...], preferred_element_type=F32)
    att = off * cst["earlier"][dirn]
    for dd in range(SUB):
        att = att + val[dd * CHUNK:(dd + 1) * CHUNK] * cst["pick"][dirn, dd]

    st = s_ref[...]
    o = _mm(att, _expand(v, hm)) + _mm_nt(qs * jnp.exp2(bc), st)
    s_ref[...] = st * jnp.exp2(total) + _mm_tn(v, kg * jnp.exp2(total - bc)) * cst["same"][...]
    return o


def _hgrn_kernel(zf_ref, zb_ref, lb_ref, llb_ref, hm_ref, same_ref, cum_ref, earlier_ref, pick_ref, hones_ref,
                 y0_ref, y1_ref, s_ref):
    @pl.when(pl.program_id(1) == 0)
    def _():
        s_ref[...] = jnp.zeros_like(s_ref)

    cst = {"headmask": hm_ref, "same": same_ref, "cum": cum_ref, "earlier": earlier_ref, "pick": pick_ref,
           "hones": hones_ref}
    for s in range(REC_BATCH):
        y0_ref[s] = _hgrn_direction(0, zf_ref[s], lb_ref[0:1], llb_ref[0:1], s_ref.at[s, 0], cst)
        y1_ref[s] = _hgrn_direction(1, zb_ref[s], lb_ref[1:2], llb_ref[1:2], s_ref.at[s, 1], cst)


def _hgrn_mixer(z_c, n_lat, lb, log_lb, cst):
    b, t, _ = z_c.shape
    n_chunks, n_lat_chunks = t // CHUNK, n_lat // CHUNK
    fwd, bwd = _chunk_maps(n_lat_chunks, n_chunks)
    shape = jax.ShapeDtypeStruct((b, t, MIX_W), F32)
    rb = REC_BATCH
    consts = [lb, log_lb, cst["headmask"], cst["same"], cst["cum"], cst["earlier"], cst["pick"], cst["hones"]]
    return pl.pallas_call(
        _hgrn_kernel,
        grid=(b // rb, n_chunks),
        in_specs=[pl.BlockSpec((rb, CHUNK, COLS_C), lambda bi, i: (bi, fwd(i), 0)),
                  pl.BlockSpec((rb, CHUNK, COLS_C), lambda bi, i: (bi, bwd(i), 0))] + [_full_spec(a) for a in consts],
        out_specs=[pl.BlockSpec((rb, CHUNK, MIX_W), lambda bi, i: (bi, fwd(i), 0)),
                   pl.BlockSpec((rb, CHUNK, MIX_W), lambda bi, i: (bi, bwd(i), 0))],
        out_shape=[shape] * 2,
        scratch_shapes=[pltpu.VMEM((rb, 2, MIX_W, MIX_W), F32)],
        compiler_params=_cp("arbitrary", "arbitrary"),
        name="hgrn2",
    )(z_c, z_c, *consts)


def _attn_kernel(tq, n_lat, zq_ref, zkv_ref, cos_ref, sin_ref, qg_ref, kg_ref, hones_ref, o_ref, k_s, v_s):
    j = pl.program_id(1)
    t = zkv_ref.shape[1]

    @pl.when(j == 0)
    def _():
        kf = zkv_ref[0][:, 256:384]
        kn = kf * lax.rsqrt(_headsum(kf * kf, hones_ref[...]) * (1.0 / HEAD_DIM) + RMS_EPS) * kg_ref[...]
        k_s[...] = _rope(kn, cos_ref[:, 0:128], sin_ref[:, 0:128]).astype(BF16)
        v_s[...] = zkv_ref[0][:, 384:512].astype(BF16)

    row0 = pl.multiple_of(j * tq, tq)
    qf = zq_ref[0][:, 0:256]
    qn = qf * lax.rsqrt(_headsum(qf * qf, hones_ref[...]) * (1.0 / HEAD_DIM) + RMS_EPS) * qg_ref[...]
    q = (_rope(qn, cos_ref[pl.ds(row0, tq), :], sin_ref[pl.ds(row0, tq), :])
         * (HEAD_DIM ** -0.5 * LOG2_E)).astype(BF16)

    def attend(key_lo, key_n):
        outs = []
        for h in range(MIX_HEADS):
            g = h // 2
            sc = lax.dot_general(q[:, h * 64:(h + 1) * 64], k_s[pl.ds(key_lo, key_n), g * 64:(g + 1) * 64],
                                 (((1,), (1,)), ((), ())), preferred_element_type=F32)
            e = jnp.exp2(sc - jnp.max(sc, axis=1, keepdims=True))
            pv = jnp.dot(e.astype(BF16), v_s[pl.ds(key_lo, key_n), g * 64:(g + 1) * 64], preferred_element_type=F32)
            outs.append(pv / jnp.sum(e, axis=1, keepdims=True))
        o_ref[0] = jnp.concatenate(outs, axis=1)

    @pl.when(row0 < n_lat)
    def _():
        attend(0, t)

    @pl.when(row0 >= n_lat)
    def _():
        attend(n_lat, t - n_lat)


def _attention_mixer(z_d, n_lat, tq, nq, cos, sin, q_g, k_g, hones):
    b, t, _ = z_d.shape
    return pl.pallas_call(
        functools.partial(_attn_kernel, tq, n_lat),
        grid=(b, nq),
        in_specs=[pl.BlockSpec((1, tq, COLS_D), lambda bi, j: (bi, j, 0)),
                  pl.BlockSpec((1, t, COLS_D), lambda bi, j: (bi, 0, 0)),
                  pl.BlockSpec((t, MIX_W), lambda bi, j: (0, 0)),
                  pl.BlockSpec((t, MIX_W), lambda bi, j: (0, 0)),
                  pl.BlockSpec((1, MIX_W), lambda bi, j: (0, 0)),
                  pl.BlockSpec((1, 128), lambda bi, j: (0, 0)),
                  _full_spec(hones)],
        out_specs=pl.BlockSpec((1, tq, MIX_W), lambda bi, j: (bi, j, 0)),
        out_shape=jax.ShapeDtypeStruct((b, nq * tq, MIX_W), F32),
        scratch_shapes=[pltpu.VMEM((t, 128), BF16), pltpu.VMEM((t, 128), BF16)],
        compiler_params=_cp("arbitrary", "arbitrary"),
        name="attention",
    )(z_d, z_d, cos, sin, q_g, k_g, hones)


def _layer_norm(x, g, b):
    mu = jnp.mean(x, axis=-1, keepdims=True)
    xc = x - mu
    var = jnp.mean(xc * xc, axis=-1, keepdims=True)
    return xc * lax.rsqrt(var + LN_EPS) * g + b


def _group_norm(y, eps, hones):
    mu = _headsum(y, hones) * (1.0 / HEAD_DIM)
    yc = y - mu
    var = _headsum(yc * yc, hones) * (1.0 / HEAD_DIM)
    return yc * lax.rsqrt(var + eps)


def _merge_kernel(alpha, n_lat_tiles, xl_ref, xc_ref, mod_ref, ry0, ry1, rbonus, rgate, ty0, ty1, tg, hy0, hy1, hg, at_ref,
                  rln_g, rln_b, tn_g, tn_b, hn_g, wg_ref, wb_ref, wo_ref, ln_g, ln_b, wr_ref, hones_ref,
                  x1_ref, u2_ref, aff_ref):
    d = xl_ref.shape[-1]
    mod = mod_ref[0, 0]
    x = jnp.where(pl.program_id(1) < n_lat_tiles, xl_ref[0], xc_ref[0])
    u = (x * (1.0 + mod[:, d:2 * d]) + mod[:, 0:d]).astype(BF16)
    hy = hy0[0] + hy1[0]
    hones = hones_ref[...]
    branches = (
        (_group_norm(ry0[0] + ry1[0], RWKV_GN_EPS, hones) * rln_g[...] + rln_b[...] + rbonus[0]) * rgate[0],
        (_group_norm(ty0[0] + ty1[0], LN_EPS, hones) * tn_g[...] + tn_b[...]) * _silu(tg[0]),
        hy * lax.rsqrt(_headsum(hy * hy, hones) * (1.0 / HEAD_DIM) + RMS_EPS) * hn_g[...] * _silu(hg[0]),
        at_ref[0],
    )
    merged = None
    for i, br in enumerate(branches):
        term = _sigmoid(jnp.dot(u, wg_ref[i], preferred_element_type=F32)) * _mm(br, wb_ref[i])
        merged = term if merged is None else merged + term
    mix = _mm(merged, wo_ref[...])
    x1 = _layer_norm(alpha * x + mod[:, 2 * d:3 * d] * mix, ln_g[...], ln_b[...])
    x1_ref[0] = x1
    u2 = (x1 * (1.0 + mod[:, 4 * d:5 * d]) + mod[:, 3 * d:4 * d]).astype(BF16)
    u2_ref[0] = u2
    logits = lax.dot_general(wr_ref[...], u2, (((1,), (1,)), ((), ())), preferred_element_type=F32)
    e = jnp.exp(logits - jnp.max(logits, axis=0, keepdims=True))
    aff_ref[0] = e / jnp.sum(e, axis=0, keepdims=True)


def _merge(alpha, x_lat, x_ctx, modsel, rw, rt, z_b, hg, z_c, att, prm, tm, n_lat_tiles, n_tiles):
    b, _, d = x_lat.shape
    rows = n_tiles * tm

    def tile(w):
        return pl.BlockSpec((1, tm, w), lambda i, j: (i, j, 0))

    def colblock(k):
        return pl.BlockSpec((1, tm, MIX_W), lambda i, j: (i, j, k))

    def full(a):
        return pl.BlockSpec(a.shape, lambda i, j, _n=a.ndim: (0,) * _n)

    params = [prm["rwkv_ln_g"], prm["rwkv_ln_b"], prm["ret_norm_g"], prm["ret_norm_b"], prm["hgrn_norm_g"],
              prm["w_gate"], prm["w_branch"], prm["w_out"], prm["ln1_g"], prm["ln1_b"], prm["w_router_t"],
              prm["hones"]]
    return pl.pallas_call(
        functools.partial(_merge_kernel, alpha, n_lat_tiles),
        grid=(b, n_tiles),
        in_specs=_two_stream_specs(tm, d, n_lat_tiles)
        + [pl.BlockSpec((1, 1, 1, 6 * d), lambda i, j: (i, (j >= n_lat_tiles).astype(jnp.int32), 0, 0))]
        + [tile(MIX_W)] * 4 + [tile(MIX_W)] * 2 + [colblock(3)] + [tile(MIX_W)] * 2 + [colblock(4)] + [tile(MIX_W)]
        + [full(a) for a in params],
        out_specs=[tile(d), tile(d), pl.BlockSpec((1, N_EXPERTS, tm), lambda i, j: (i, 0, j))],
        out_shape=[jax.ShapeDtypeStruct((b, rows, d), F32), jax.ShapeDtypeStruct((b, rows, d), BF16),
                   jax.ShapeDtypeStruct((b, N_EXPERTS, rows), F32)],
        compiler_params=_cp("arbitrary", "arbitrary"),
        name="merge",
    )(x_lat, x_ctx, modsel, *rw, rt[0], rt[1], z_b, hg[0], hg[1], z_c, att, *params)


def _route_kernel(start, n, cap, aff_ref, code_ref, gate_ref):
    a = aff_ref[0][:, start:start + n]
    bits = pltpu.bitcast(a, jnp.int32)

    def bisect(_, lohi):
        lo, hi = lohi
        mid = lo + ((hi - lo + 1) >> 1)
        ok = jnp.sum(jnp.where(bits >= mid, 1.0, 0.0), axis=1, keepdims=True) >= cap
        return jnp.where(ok, mid, lo), jnp.where(ok, hi, mid - 1)

    lo0 = jnp.zeros((N_EXPERTS, 1), jnp.int32)
    thr, _ = lax.fori_loop(0, 31, bisect, (lo0, jnp.full((N_EXPERTS, 1), 0x7F800000, jnp.int32)))
    gt, eq = bits > thr, bits == thr
    need = cap - jnp.sum(jnp.where(gt, 1.0, 0.0), axis=1, keepdims=True)
    blk = min(n, ROUTE_BLOCK)
    upper = jnp.where(_iota((blk, blk), 0) <= _iota((blk, blk), 1), 1.0, 0.0).astype(BF16)

    def prefix_count(m):
        parts, running = [], jnp.zeros((N_EXPERTS, 1), F32)
        for o in range(0, n, blk):
            pre = jnp.dot(jnp.where(m[:, o:o + blk], 1.0, 0.0).astype(BF16), upper, preferred_element_type=F32)
            parts.append(pre + running)
            running = running + pre[:, blk - 1:blk]
        return jnp.concatenate(parts, axis=1)

    sel = gt | (eq & (prefix_count(eq) <= need))
    rank = prefix_count(sel) - 1.0
    code_ref[0] = jnp.where(sel, rank.astype(jnp.int32), -1)
    gate_ref[0] = a


def _route(aff, start, n, cap):
    b, e, rows = aff.shape
    code, gate = pl.pallas_call(
        functools.partial(_route_kernel, start, n, cap),
        grid=(b,),
        in_specs=[pl.BlockSpec((1, e, rows), lambda i: (i, 0, 0))],
        out_specs=[pl.BlockSpec((1, e, n), lambda i: (i, 0, 0))] * 2,
        out_shape=[jax.ShapeDtypeStruct((b, e, n), jnp.int32), jax.ShapeDtypeStruct((b, e, n), F32)],
        compiler_params=_cp("arbitrary"),
        name="route",
    )(aff)
    return code.reshape(b, e, 1, n), gate.reshape(b, e, 1, n)


def _moe_kernel(bg, cap_pad, code_ref, gate_ref, u_ref, w1_ref, w3_ref, w2_ref, o_ref):
    @pl.when(pl.program_id(1) == 0)
    def _():
        o_ref[...] = jnp.zeros_like(o_ref)

    n = u_ref.shape[1]
    slot = _iota((cap_pad, n), 0)
    hits = [slot == code_ref[s, 0] for s in range(bg)]
    xs = jnp.concatenate([jnp.dot(jnp.where(h, 1.0, 0.0).astype(BF16), u_ref[s], preferred_element_type=F32)
                          for s, h in enumerate(hits)], axis=0).astype(BF16)
    hmid = _silu(jnp.dot(xs, w1_ref[0], preferred_element_type=F32)) * jnp.dot(xs, w3_ref[0],
                                                                               preferred_element_type=F32)
    y = jnp.dot(hmid.astype(BF16), w2_ref[0], preferred_element_type=F32)
    for s, h in enumerate(hits):
        weights = jnp.where(h, gate_ref[s, 0], 0.0)
        o_ref[s] += _mm_tn(weights, y[s * cap_pad:(s + 1) * cap_pad])


def _moe(code, gate, u2, row_block, n, cap, bg, w1, w3, w2):
    b = u2.shape[0]
    d = u2.shape[-1]
    cap_pad = max(cap, 128)
    f = w1.shape[-1]
    return pl.pallas_call(
        functools.partial(_moe_kernel, bg, cap_pad),
        grid=(b // bg, N_EXPERTS),
        in_specs=[pl.BlockSpec((bg, 1, 1, n), lambda i, e: (i, e, 0, 0)),
                  pl.BlockSpec((bg, 1, 1, n), lambda i, e: (i, e, 0, 0)),
                  pl.BlockSpec((bg, n, d), lambda i, e: (i, row_block, 0)),
                  pl.BlockSpec((1, d, f), lambda i, e: (e, 0, 0)),
                  pl.BlockSpec((1, d, f), lambda i, e: (e, 0, 0)),
                  pl.BlockSpec((1, f, d), lambda i, e: (e, 0, 0))],
        out_specs=pl.BlockSpec((bg, n, d), lambda i, e: (i, 0, 0)),
        out_shape=jax.ShapeDtypeStruct((b, n, d), F32),
        compiler_params=_cp("arbitrary", "arbitrary"),
        name="expert_ffn",
    )(code, gate, u2, w1, w3, w2)


def _ln2_kernel(alpha, x_ref, f_ref, mod_ref, g_ref, b_ref, o_ref):
    d = x_ref.shape[-1]
    o_ref[0] = _layer_norm(alpha * x_ref[0] + mod_ref[0, 0][:, 5 * d:6 * d] * f_ref[0], g_ref[...], b_ref[...])


def _ln2(alpha, x1, x1_tile0, ffn, modsel, seg, g, bias, tm):
    b, rows, d = ffn.shape
    return pl.pallas_call(
        functools.partial(_ln2_kernel, alpha),
        grid=(b, rows // tm),
        in_specs=[pl.BlockSpec((1, tm, d), lambda i, j: (i, j + x1_tile0, 0)),
                  pl.BlockSpec((1, tm, d), lambda i, j: (i, j, 0)),
                  pl.BlockSpec((1, 1, 1, 6 * d), lambda i, j: (i, seg, 0, 0)),
                  pl.BlockSpec((1, d), lambda i, j: (0, 0)),
                  pl.BlockSpec((1, d), lambda i, j: (0, 0))],
        out_specs=pl.BlockSpec((1, tm, d), lambda i, j: (i, j, 0)),
        out_shape=jax.ShapeDtypeStruct((b, rows, d), F32),
        compiler_params=_cp("arbitrary", "arbitrary"),
        name="ln2",
    )(x1, ffn, modsel, g, bias)


def _rope_tables(n_lat, n_ctx):
    rows = n_lat // GRID_W
    row = jnp.repeat(jnp.arange(rows), GRID_W)
    col = jnp.tile(jnp.arange(GRID_W), rows)
    n_freq = HEAD_DIM // 4
    inv = ROPE_BASE ** (-jnp.arange(n_freq, dtype=F32) / n_freq)
    ang = jnp.concatenate([row[:, None] * inv, col[:, None] * inv], axis=-1)
    cos, sin = jnp.cos(ang), jnp.sin(ang)
    cos64 = jnp.concatenate([cos, cos], axis=-1)
    sin64 = jnp.concatenate([-sin, sin], axis=-1)
    cos64 = jnp.concatenate([cos64, jnp.ones((n_ctx, HEAD_DIM), F32)], axis=0)
    sin64 = jnp.concatenate([sin64, jnp.zeros((n_ctx, HEAD_DIM), F32)], axis=0)
    return jnp.tile(cos64, (1, MIX_HEADS)), jnp.tile(sin64, (1, MIX_HEADS))


def kernel(x, c, ctx, c_ctx, w_mod, b_mod, w_in, rwkv_mu, rwkv_w0, rwkv_w2, rwkv_a0, rwkv_a2, rwkv_g2, rwkv_kk, rwkv_ka, rwkv_rk, rwkv_ln_g, rwkv_ln_b, ret_decay, ret_norm_g, ret_norm_b, hgrn_lb, hgrn_norm_g, attn_q_g, attn_k_g, w_gate, w_branch, w_out, ln1_g, ln1_b, w_router, w_e1, w_e3, w_e2, ln2_g, ln2_b):
    bsz, n_lat, d = x.shape
    n_ctx = ctx.shape[1]
    depth = w_mod.shape[0]
    t = n_lat + n_ctx
    tm = min(256, n_ctx)
    assert n_lat % tm == 0 and n_ctx % tm == 0 and tm % CHUNK == 0 and n_lat % n_ctx == 0 and bsz % REC_BATCH == 0
    n_lat_tiles, n_tiles = n_lat // tm, t // tm
    alpha = (2 * depth) ** 0.25

    cos, sin = _rope_tables(n_lat, n_ctx)
    lb_w = jax.nn.softmax(hgrn_lb.astype(F32), axis=1)
    lower = jnp.cumsum(lb_w, axis=1) - lb_w[:, :1]
    log_lower = jnp.log(jnp.maximum(lower, HGRN_LB_FLOOR))
    log_gamma = jax.nn.log_sigmoid(ret_decay.astype(F32))
    cst = _scan_constants()

    rows = 8 * ((bsz + 1 + 7) // 8)
    cc = jnp.zeros((rows, d), F32).at[:bsz].set(c).at[bsz].set(c_ctx)
    mods = _modulation(cc, w_mod.astype(BF16), b_mod)

    x_lat, x_ctx = x, ctx
    row = lambda a: a.reshape(1, -1)
    for l in range(depth):
        need_ctx = l < depth - 1
        modsel = jnp.stack([mods[l, :bsz], jnp.broadcast_to(mods[l, bsz], (bsz, 6 * d))],
                           axis=1).reshape(bsz, 2, 1, 6 * d)
        z_a, z_b, z_c, z_d = _in_projection(x_lat, x_ctx, modsel, w_in[l].astype(BF16), tm, n_lat_tiles)
        rw = _rwkv_mixer(z_a, n_lat, {
            "mu": row(rwkv_mu[l]), "w0": rwkv_w0[l], "w2": rwkv_w2[l].astype(BF16), "a0": rwkv_a0[l],
            "a2": rwkv_a2[l].astype(BF16), "g2": rwkv_g2[l].astype(BF16), "k_k": row(rwkv_kk[l]),
            "k_a": row(rwkv_ka[l]), "r_k": row(rwkv_rk[l])}, cst)
        rt = _retention_mixer(z_b, n_lat, cos, sin, log_gamma[l], cst)
        hg = _hgrn_mixer(z_c, n_lat, lower[:, l], log_lower[:, l], cst)
        n_out_tiles = n_tiles if need_ctx else n_lat_tiles
        att = _attention_mixer(z_d, n_lat, tm, n_out_tiles, cos, sin,
                               row(jnp.tile(attn_q_g[l], MIX_HEADS)), row(jnp.tile(attn_k_g[l], 2)), cst["hones"])
        x1, u2, aff = _merge(alpha, x_lat, x_ctx, modsel, rw, rt, z_b, hg, z_c, att, {
            "rwkv_ln_g": row(rwkv_ln_g[l]), "rwkv_ln_b": row(rwkv_ln_b[l]), "ret_norm_g": row(ret_norm_g[l]),
            "ret_norm_b": row(ret_norm_b[l]), "hgrn_norm_g": row(hgrn_norm_g[l]),
            "w_gate": w_gate[l].astype(BF16), "w_branch": w_branch[l].astype(BF16), "w_out": w_out[l].astype(BF16),
            "ln1_g": row(ln1_g[l]), "ln1_b": row(ln1_b[l]), "w_router_t": w_router[l].T.astype(BF16),
            "hones": cst["hones"]},
            tm, n_lat_tiles, n_out_tiles)
        w1, w3, w2 = w_e1[l].astype(BF16), w_e3[l].astype(BF16), w_e2[l].astype(BF16)
        cap_lat = EC_CAPACITY * n_lat // N_EXPERTS
        code, gate = _route(aff, 0, n_lat, cap_lat)
        ffn = _moe(code, gate, u2, 0, n_lat, cap_lat, 1, w1, w3, w2)
        x_lat = _ln2(alpha, x1, 0, ffn, modsel, 0, row(ln2_g[l]), row(ln2_b[l]), tm)
        if not need_ctx:
            return x_lat
        cap_ctx = EC_CAPACITY * n_ctx // N_EXPERTS
        code_c, gate_c = _route(aff, n_lat, n_ctx, cap_ctx)
        ffn_c = _moe(code_c, gate_c, u2, n_lat // n_ctx, n_ctx, cap_ctx, 8 if bsz % 8 == 0 else 1, w1, w3, w2)
        x_ctx = _ln2(alpha, x1, n_lat_tiles, ffn_c, modsel, 1, row(ln2_g[l]), row(ln2_b[l]), tm)
    return x_lat
```

```python
import functools

import jax
import jax.numpy as jnp
import numpy as np
from jax import lax
from jax.experimental import pallas as pl
from jax.experimental.pallas import tpu as pltpu

F32 = jnp.float32
BF16 = jnp.bfloat16

HEAD_DIM = 64
MIX_HEADS = 4
MIX_W = MIX_HEADS * HEAD_DIM
GRID_W = 64
ROPE_BASE = 10000.0
N_EXPERTS = 16
EC_CAPACITY = 2
LN_EPS = 1e-5
RMS_EPS = 1e-6
RWKV_GN_EPS = HEAD_DIM * 1e-5
HGRN_LB_FLOOR = 1e-20
LOG2_E = 1.4426950408889634
RWKV_LORA = 64
COLS_A, COLS_B, COLS_C, COLS_D = 1152, 1024, 1280, 512
CHUNK = 64
SUB_SHIFT = 3
SUB = 1 << SUB_SHIFT
EXP = MIX_HEADS * CHUNK
REC_BATCH = 8
ROUTE_BLOCK = 256
VMEM_LIMIT = 56 * 1024 * 1024


def _cp(*sem):
    return pltpu.CompilerParams(dimension_semantics=sem, vmem_limit_bytes=VMEM_LIMIT)


def _mm(a, b):
    return jnp.dot(a.astype(BF16), b.astype(BF16), preferred_element_type=F32)


def _mm_nt(a, b):
    return lax.dot_general(a.astype(BF16), b.astype(BF16), (((1,), (1,)), ((), ())), preferred_element_type=F32)


def _mm_tn(a, b):
    return lax.dot_general(a.astype(BF16), b.astype(BF16), (((0,), (0,)), ((), ())), preferred_element_type=F32)


def _split(x):
    hi = x.astype(BF16)
    lo = (x - hi.astype(F32)).astype(BF16)
    return hi, lo


def _mm_mask_l(mask_bf16, x):
    hi, lo = _split(x)
    return (jnp.dot(mask_bf16, hi, preferred_element_type=F32) + jnp.dot(mask_bf16, lo, preferred_element_type=F32))


def _mm_mask_r(x, mask_bf16):
    hi, lo = _split(x)
    return (jnp.dot(hi, mask_bf16, preferred_element_type=F32) + jnp.dot(lo, mask_bf16, preferred_element_type=F32))


def _iota(shape, dim):
    return lax.broadcasted_iota(jnp.int32, shape, dim)


def _headsum(x, hones):
    n = x.shape[-1]
    return _mm_mask_r(x, hones[0:n, 0:n])


def _sigmoid(x):
    return 1.0 / (1.0 + jnp.exp(-x))


def _silu(x):
    return x * _sigmoid(x)


def _softplus(x):
    return jnp.maximum(x, 0.0) + jnp.log(1.0 + jnp.exp(-jnp.abs(x)))


def _expand(x, headmask):
    tiled = jnp.concatenate([x.astype(BF16)] * MIX_HEADS, axis=0)
    return pltpu.bitcast(pltpu.bitcast(tiled, jnp.uint32) & headmask, BF16)


def _scan_constants():
    rows = np.arange(EXP)
    head = rows // CHUNK
    same = head[:, None] == head[None, :]
    lane_head = np.arange(MIX_W) // HEAD_DIM
    lane_tok = np.arange(MIX_W) % HEAD_DIM
    tri, cum, pick = [], [], []
    for dirn in (0, 1):
        flip = (lambda a: a) if dirn == 0 else (lambda a: CHUNK - 1 - a)
        p64, pl64 = flip(np.arange(CHUNK))[:, None], flip(lane_tok)[None, :]
        tri.append(np.stack([pl64 < p64, pl64 <= p64]))
        cum.append(p64.T <= p64)
        pick.append(np.stack([((p64 - pl64) == dd) & ((p64 >> SUB_SHIFT) == (pl64 >> SUB_SHIFT)) for dd in range(SUB)]))
    f = lambda a: jnp.asarray(np.asarray(a, np.float32))
    return {
        "headmask": jnp.asarray(np.where(head[::2, None] == lane_head[None, :], 0xFFFFFFFF, 0).astype(np.uint32)),
        "same": f(same),
        "tri": f(np.stack(tri)),
        "subcols": f((lane_tok[None, :] >> SUB_SHIFT) == np.arange(CHUNK // SUB)[:, None]),
        "cum": f(np.stack(cum)).astype(BF16),
        "pick": f(np.stack(pick)),
        "hones": f(lane_head[:, None] == lane_head[None, :]).astype(BF16),
    }


def _round_robin(chains):
    results = [None] * len(chains)
    live = list(range(len(chains)))
    while live:
        for idx in list(live):
            try:
                next(chains[idx])
            except StopIteration as done:
                results[idx] = done.value
                live.remove(idx)
    return results


def _full_spec(a):
    return pl.BlockSpec(a.shape, lambda *_, _n=a.ndim: (0,) * _n)


def _rope(x, cos, sin):
    blocks = [x[:, i:i + 128] for i in range(0, x.shape[-1], 128)]
    fwd = jnp.concatenate([pltpu.roll(b, 32, 1) for b in blocks], axis=1)
    bwd = jnp.concatenate([pltpu.roll(b, 128 - 32, 1) for b in blocks], axis=1)
    first = (_iota(x.shape, 1) & 63) < 32
    return x * cos + jnp.where(first, bwd, fwd) * sin


def _mod_kernel(c_ref, w_ref, b_ref, o_ref):
    o_ref[0] = _mm(_silu(c_ref[...]), w_ref[0]) + b_ref[0]


def _modulation(cc, w_mod, b_mod):
    depth, d, d6 = w_mod.shape
    tn = d6 // 4
    rows = cc.shape[0]
    return pl.pallas_call(
        _mod_kernel,
        grid=(depth, d6 // tn),
        in_specs=[pl.BlockSpec((rows, d), lambda l, j: (0, 0)),
                  pl.BlockSpec((1, d, tn), lambda l, j: (l, 0, j)),
                  pl.BlockSpec((1, 1, tn), lambda l, j: (l, 0, j))],
        out_specs=pl.BlockSpec((1, rows, tn), lambda l, j: (l, 0, j)),
        out_shape=jax.ShapeDtypeStruct((depth, rows, d6), F32),
        compiler_params=_cp("arbitrary", "arbitrary"),
        name="modulation",
    )(cc, w_mod, b_mod.reshape(depth, 1, d6))


def _inproj_kernel(n_lat_tiles, xl_ref, xc_ref, mod_ref, w_ref, za_ref, zb_ref, zc_ref, zd_ref):
    d = xl_ref.shape[-1]
    mod = mod_ref[0, 0]
    x = jnp.where(pl.program_id(1) < n_lat_tiles, xl_ref[0], xc_ref[0])
    u = (x * (1.0 + mod[:, d:2 * d]) + mod[:, 0:d]).astype(BF16)
    o = 0
    for ref, n in ((za_ref, COLS_A), (zb_ref, COLS_B), (zc_ref, COLS_C), (zd_ref, COLS_D)):
        ref[0] = jnp.dot(u, w_ref[:, o:o + n], preferred_element_type=F32)
        o += n


def _two_stream_specs(tm, d, n_lat_tiles):
    return [pl.BlockSpec((1, tm, d), lambda i, j: (i, jnp.minimum(j, n_lat_tiles - 1), 0)),
            pl.BlockSpec((1, tm, d), lambda i, j: (i, jnp.maximum(j - n_lat_tiles, 0), 0))]


def _in_projection(x_lat, x_ctx, modsel, w_in, tm, n_lat_tiles):
    b, n_lat, d = x_lat.shape
    t = n_lat + x_ctx.shape[1]
    d_in = w_in.shape[-1]
    cols = (COLS_A, COLS_B, COLS_C, COLS_D)
    return pl.pallas_call(
        functools.partial(_inproj_kernel, n_lat_tiles),
        grid=(b, t // tm),
        in_specs=_two_stream_specs(tm, d, n_lat_tiles) + [
            pl.BlockSpec((1, 1, 1, 6 * d), lambda i, j: (i, (j >= n_lat_tiles).astype(jnp.int32), 0, 0)),
            pl.BlockSpec((d, d_in), lambda i, j: (0, 0))],
        out_specs=[pl.BlockSpec((1, tm, n), lambda i, j: (i, j, 0)) for n in cols],
        out_shape=[jax.ShapeDtypeStruct((b, t, n), F32) for n in cols],
        compiler_params=_cp("arbitrary", "arbitrary"),
        name="in_projection",
    )(x_lat, x_ctx, modsel, w_in)


def _chunk_maps(n_lat_chunks, n_chunks):
    n_ctx_chunks = n_chunks - n_lat_chunks

    def fwd(i):
        return jnp.where(i < n_ctx_chunks, n_lat_chunks + i, i - n_ctx_chunks)

    def bwd(i):
        return jnp.where(i < n_ctx_chunks, n_chunks - 1 - i, n_chunks - 1 - i)

    return fwd, bwd


def _rwkv_direction(dirn, zc, prev_row, next_row, s_ref, p, cst):
    hm = cst["headmask"][...]
    rowi = _iota(zc.shape, 0)
    up = jnp.where(rowi == 0, prev_row, pltpu.roll(zc, 1, 0))
    dn = jnp.where(rowi == CHUNK - 1, next_row, pltpu.roll(zc, CHUNK - 1, 0))
    zs = zc + p["mu"] * (0.5 * (up + dn) - zc)
    r, k, v = zs[:, 0:256], zs[:, 256:512], zs[:, 512:768]
    zw = zs[:, 768 + RWKV_LORA * dirn:768 + RWKV_LORA * (dirn + 1)]
    za = zs[:, 896 + RWKV_LORA * dirn:896 + RWKV_LORA * (dirn + 1)]
    w = p["w0"][dirn:dirn + 1] + _mm(jnp.tanh(zw), p["w2"][dirn])
    lw = -jnp.exp(-_softplus(-w) - 0.5)
    a = _sigmoid(p["a0"][dirn:dirn + 1] + _mm(za, p["a2"][dirn]))
    kkf = k * p["k_k"]
    kk = kkf * lax.rsqrt(_headsum(kkf * kkf, cst["hones"][...]) + 1e-12)
    kd = k * (1.0 + (a - 1.0) * p["k_a"])
    bv = kk * a

    cum = _mm_mask_l(cst["cum"][dirn], lw)
    total = cum[CHUNK - 1:CHUNK] if dirn == 0 else cum[0:1]
    inv = jnp.exp(-cum)
    tail = jnp.exp(total - cum)
    lhs = jnp.concatenate([kk * jnp.exp(cum - lw), r * jnp.exp(cum)], axis=0)
    g_k = _mm_nt(lhs, _expand(kd * inv, hm))
    g_b = _mm_nt(lhs, _expand(bv * inv, hm))
    yield
    strict, incl = cst["tri"][dirn, 0], cst["tri"][dirn, 1]
    m_b = g_b[0:CHUNK] * strict
    n_b = g_b[CHUNK:2 * CHUNK] * incl
    mn_k = g_k * jnp.concatenate([strict, incl], axis=0)

    st = s_ref[...]
    carry = _mm_nt(lhs, st) + _mm(mn_k, _expand(v, hm))
    x = carry[0:CHUNK]
    yield
    x = x - _mm(m_b, _expand(x, hm))
    pw = _mm(m_b, _expand(m_b, hm))
    yield
    for step in range(5):
        x = x + _mm(pw, _expand(x, hm))
        if step < 4:
            pw = _mm(pw, _expand(pw, hm))
        yield
    y = carry[CHUNK:2 * CHUNK] - _mm(n_b, _expand(x, hm))
    upd = _mm_tn(jnp.concatenate([v, x], axis=0), jnp.concatenate([kd * tail, -(bv * tail)], axis=0))
    s_ref[...] = st * jnp.exp(total) + upd * cst["same"][...]
    return y, r, k, v, zs[:, 1024:1152]


def _rwkv_kernel(n_lat_chunks, n_chunks,
                 zf_ref, zfp_ref, zfn_ref, zb_ref, zbp_ref, zbn_ref,
                 mu_ref, w0_ref, w2_ref, a0_ref, a2_ref, g2_ref, kk_ref, ka_ref, rk_ref,
                 hm_ref, same_ref, tri_ref, cum_ref, hones_ref,
                 y0_ref, y1_ref, bonus_ref, gate_ref, s_ref):
    i = pl.program_id(1)
    fwd, bwd = _chunk_maps(n_lat_chunks, n_chunks)

    @pl.when(i == 0)
    def _():
        s_ref[...] = jnp.zeros_like(s_ref)

    p = {"mu": mu_ref[...], "w0": w0_ref[...], "w2": w2_ref, "a0": a0_ref[...], "a2": a2_ref,
         "k_k": kk_ref[...], "k_a": ka_ref[...]}
    cst = {"headmask": hm_ref, "same": same_ref, "tri": tri_ref, "cum": cum_ref, "hones": hones_ref}
    chains = []
    for s in range(REC_BATCH):
        for dirn, (z_ref, zp_ref, zn_ref) in enumerate(((zf_ref, zfp_ref, zfn_ref), (zb_ref, zbp_ref, zbn_ref))):
            c = fwd(i) if dirn == 0 else bwd(i)
            first = jnp.logical_or(c == 0, c == n_lat_chunks)
            last = jnp.logical_or(c == n_lat_chunks - 1, c == n_chunks - 1)
            prev_row = jnp.where(first, 0.0, zp_ref[s][7:8, :])
            next_row = jnp.where(last, 0.0, zn_ref[s][0:1, :])
            chains.append(_rwkv_direction(dirn, z_ref[s], prev_row, next_row, s_ref.at[s, dirn], p, cst))
    for idx, (y, r, k, v, zg) in enumerate(_round_robin(chains)):
        s, dirn = divmod(idx, 2)
        if dirn == 0:
            y0_ref[s] = y
            bonus_ref[s] = _headsum(r * k * rk_ref[...], hones_ref[...]) * v
            gate_ref[s] = _mm(_sigmoid(zg), g2_ref[...])
        else:
            y1_ref[s] = y


def _rwkv_mixer(z_a, n_lat, prm, cst):
    b, t, _ = z_a.shape
    n_chunks, n_lat_chunks = t // CHUNK, n_lat // CHUNK
    fwd, bwd = _chunk_maps(n_lat_chunks, n_chunks)
    per8 = CHUNK // 8
    last8 = t // 8 - 1
    rb = REC_BATCH

    def zspecs(cm):
        return [pl.BlockSpec((rb, CHUNK, COLS_A), lambda bi, i: (bi, cm(i), 0)),
                pl.BlockSpec((rb, 8, COLS_A), lambda bi, i: (bi, jnp.maximum(cm(i) * per8 - 1, 0), 0)),
                pl.BlockSpec((rb, 8, COLS_A), lambda bi, i: (bi, jnp.minimum((cm(i) + 1) * per8, last8), 0))]

    params = [prm["mu"], prm["w0"], prm["w2"], prm["a0"], prm["a2"], prm["g2"], prm["k_k"], prm["k_a"], prm["r_k"],
              cst["headmask"], cst["same"], cst["tri"], cst["cum"], cst["hones"]]
    yspec_f = pl.BlockSpec((rb, CHUNK, MIX_W), lambda bi, i: (bi, fwd(i), 0))
    yspec_b = pl.BlockSpec((rb, CHUNK, MIX_W), lambda bi, i: (bi, bwd(i), 0))
    shape = jax.ShapeDtypeStruct((b, t, MIX_W), F32)
    return pl.pallas_call(
        functools.partial(_rwkv_kernel, n_lat_chunks, n_chunks),
        grid=(b // rb, n_chunks),
        in_specs=zspecs(fwd) + zspecs(bwd) + [_full_spec(a) for a in params],
        out_specs=[yspec_f, yspec_b, yspec_f, yspec_f],
        out_shape=[shape] * 4,
        scratch_shapes=[pltpu.VMEM((rb, 2, MIX_W, MIX_W), F32)],
        compiler_params=_cp("arbitrary", "arbitrary"),
        name="rwkv7",
    )(z_a, z_a, z_a, z_a, z_a, z_a, *params)


def _ret_direction(dirn, zc, cos, sin, lg, intra, chunk_decay, r_ref, hm, same):
    q = _rope(zc[:, 0:256], cos, sin)
    k = _rope(zc[:, 256:512], cos, sin) * HEAD_DIM ** -0.5
    v = zc[:, 512:768]
    pos = _iota((CHUNK, MIX_W), 0).astype(F32)
    if dirn == 1:
        pos = (CHUNK - 1.0) - pos
    sc = _mm_nt(q, _expand(k, hm)) * intra
    rs = r_ref[...]
    o = _mm(sc, _expand(v, hm)) + _mm(q * jnp.exp(lg * (pos + 1.0)), rs)
    r_ref[...] = rs * chunk_decay + _mm_tn(k * jnp.exp(lg * ((CHUNK - 1.0) - pos)), v) * same
    return o


def _ret_kernel(zf_ref, zb_ref, cf_ref, sf_ref, cb_ref, sb_ref, lg_ref, intra_ref, cd_ref, hm_ref, same_ref,
                y0_ref, y1_ref, r_ref):
    @pl.when(pl.program_id(1) == 0)
    def _():
        r_ref[...] = jnp.zeros_like(r_ref)

    hm, same = hm_ref[...], same_ref[...]
    for s in range(REC_BATCH):
        y0_ref[s] = _ret_direction(0, zf_ref[s], cf_ref[...], sf_ref[...], lg_ref[0:1], intra_ref[0], cd_ref[0],
                                   r_ref.at[s, 0], hm, same)
        y1_ref[s] = _ret_direction(1, zb_ref[s], cb_ref[...], sb_ref[...], lg_ref[1:2], intra_ref[1], cd_ref[1],
                                   r_ref.at[s, 1], hm, same)


def _retention_tables(log_gamma, cst):
    lane_tok = np.arange(MIX_W) % HEAD_DIM
    dist = np.abs(np.arange(CHUNK)[:, None] - lane_tok[None, :]).astype(np.float32)
    lg_lanes = jnp.repeat(log_gamma, HEAD_DIM, axis=-1)
    intra = jnp.exp(lg_lanes[:, None, :] * dist) * cst["tri"][:, 1]
    return intra, jnp.exp(lg_lanes[:, :, None] * float(CHUNK)) * cst["same"]


def _retention_mixer(z_b, n_lat, cos, sin, log_gamma, cst):
    b, t, _ = z_b.shape
    n_chunks, n_lat_chunks = t // CHUNK, n_lat // CHUNK
    fwd, bwd = _chunk_maps(n_lat_chunks, n_chunks)
    shape = jax.ShapeDtypeStruct((b, t, MIX_W), F32)
    rb = REC_BATCH
    intra, chunk_decay = _retention_tables(log_gamma, cst)
    lg_lanes = jnp.repeat(log_gamma, HEAD_DIM, axis=-1)

    def zs(cm):
        return pl.BlockSpec((rb, CHUNK, COLS_B), lambda bi, i: (bi, cm(i), 0))

    def ts(cm):
        return pl.BlockSpec((CHUNK, MIX_W), lambda bi, i: (cm(i), 0))

    consts = [lg_lanes, intra, chunk_decay, cst["headmask"], cst["same"]]
    return pl.pallas_call(
        _ret_kernel,
        grid=(b // rb, n_chunks),
        in_specs=[zs(fwd), zs(bwd), ts(fwd), ts(fwd), ts(bwd), ts(bwd)] + [_full_spec(a) for a in consts],
        out_specs=[pl.BlockSpec((rb, CHUNK, MIX_W), lambda bi, i: (bi, fwd(i), 0)),
                   pl.BlockSpec((rb, CHUNK, MIX_W), lambda bi, i: (bi, bwd(i), 0))],
        out_shape=[shape] * 2,
        scratch_shapes=[pltpu.VMEM((rb, 2, MIX_W, MIX_W), F32)],
        compiler_params=_cp("arbitrary", "arbitrary"),
        name="retention",
    )(z_b, z_b, cos, sin, cos, sin, *consts)


def _hgrn_direction(dirn, zc, lb, log_lb, s_ref, cst):
    hm = cst["headmask"][...]
    qs = _silu(zc[:, 0:256])
    fz = zc[:, 256 + MIX_W * dirn:256 + MIX_W * (dirn + 1)]
    v = zc[:, 768:1024]
    tail_term = jnp.log(1.0 + jnp.exp(-jnp.abs(fz)))
    ls_pos = -(jnp.maximum(-fz, 0.0) + tail_term)
    ls_neg = -(jnp.maximum(fz, 0.0) + tail_term)
    p1, p2 = ls_pos, log_lb + ls_neg
    log_f = jnp.maximum(p1, p2) + jnp.log(1.0 + jnp.exp(-jnp.abs(p1 - p2)))
    kg = (1.0 - lb) * _sigmoid(-fz)
    bc = _mm_mask_l(cst["cum"][dirn], log_f * LOG2_E)
    total = bc[CHUNK - 1:CHUNK] if dirn == 0 else bc[0:1]
    excl = bc - log_f * LOG2_E

    nsub = CHUNK // SUB
    first_tok = [b * SUB if dirn == 0 else b * SUB + SUB - 1 for b in range(nsub)]
    last_tok = [b * SUB + SUB - 1 if dirn == 0 else b * SUB for b in range(nsub)]
    e_start = [excl[i:i + 1] for i in first_tok]
    e_end = [bc[i:i + 1] for i in last_tok]
    row_ref = jnp.concatenate([jnp.broadcast_to(e, (SUB, MIX_W)) for e in e_start], axis=0)
    key_ref = jnp.concatenate([jnp.broadcast_to(e, (SUB, MIX_W)) for e in e_end], axis=0)
    q_rel = qs * jnp.exp2(bc - row_ref)
    k_rel = kg * jnp.exp2(key_ref - bc)
    pairs = [(bi, bj) for bi in range(nsub) for bj in range(nsub) if (bj < bi if dirn == 0 else bj > bi)]
    lhs = jnp.concatenate([q_rel[bi * SUB:(bi + 1) * SUB] * jnp.exp2(e_start[bi] - e_end[bj]) for bi, bj in pairs],
                          axis=0)
    g = _mm_nt(lhs, _expand(k_rel, hm))
    acc = [None] * nsub
    for idx, (bi, bj) in enumerate(pairs):
        part = g[idx * SUB:(idx + 1) * SUB] * cst["subcols"][bj:bj + 1]
        acc[bi] = part if acc[bi] is None else acc[bi] + part
    att = jnp.concatenate([a if a is not None else jnp.zeros((SUB, MIX_W), F32) for a in acc], axis=0)
    prods = []
    for dd in range(SUB):
        sh = dd if dirn == 0 else (CHUNK - dd) % CHUNK
        kr = kg if dd == 0 else pltpu.roll(kg, sh, 0)
        br = bc if dd == 0 else pltpu.roll(bc, sh, 0)
        prods.append((qs * kr * jnp.exp2(jnp.minimum(bc - br, 0.0))).astype(BF16))
    val = jnp.dot(jnp.concatenate(prods, axis=0), cst["hones"][...], preferred_element_type=F32)
    for dd in range(SUB):
        att = att + val[dd * CHUNK:(dd + 1) * CHUNK] * cst["pick"][dirn, dd]

    st = s_ref[...]
    o = _mm(att, _expand(v, hm)) + _mm_nt(qs * jnp.exp2(bc), st)
    s_ref[...] = st * jnp.exp2(total) + _mm_tn(v, kg * jnp.exp2(total - bc)) * cst["same"][...]
    return o


def _hgrn_kernel(zf_ref, zb_ref, lb_ref, llb_ref, hm_ref, same_ref, cum_ref, subcols_ref, pick_ref, hones_ref,
                 y0_ref, y1_ref, s_ref):
    @pl.when(pl.program_id(1) == 0)
    def _():
        s_ref[...] = jnp.zeros_like(s_ref)

    cst = {"headmask": hm_ref, "same": same_ref, "cum": cum_ref, "subcols": subcols_ref, "pick": pick_ref,
           "hones": hones_ref}
    for s in range(REC_BATCH):
        y0_ref[s] = _hgrn_direction(0, zf_ref[s], lb_ref[0:1], llb_ref[0:1], s_ref.at[s, 0], cst)
        y1_ref[s] = _hgrn_direction(1, zb_ref[s], lb_ref[1:2], llb_ref[1:2], s_ref.at[s, 1], cst)


def _hgrn_mixer(z_c, n_lat, lb, log_lb, cst):
    b, t, _ = z_c.shape
    n_chunks, n_lat_chunks = t // CHUNK, n_lat // CHUNK
    fwd, bwd = _chunk_maps(n_lat_chunks, n_chunks)
    shape = jax.ShapeDtypeStruct((b, t, MIX_W), F32)
    rb = REC_BATCH
    consts = [lb, log_lb, cst["headmask"], cst["same"], cst["cum"], cst["subcols"], cst["pick"], cst["hones"]]
    return pl.pallas_call(
        _hgrn_kernel,
        grid=(b // rb, n_chunks),
        in_specs=[pl.BlockSpec((rb, CHUNK, COLS_C), lambda bi, i: (bi, fwd(i), 0)),
                  pl.BlockSpec((rb, CHUNK, COLS_C), lambda bi, i: (bi, bwd(i), 0))] + [_full_spec(a) for a in consts],
        out_specs=[pl.BlockSpec((rb, CHUNK, MIX_W), lambda bi, i: (bi, fwd(i), 0)),
                   pl.BlockSpec((rb, CHUNK, MIX_W), lambda bi, i: (bi, bwd(i), 0))],
        out_shape=[shape] * 2,
        scratch_shapes=[pltpu.VMEM((rb, 2, MIX_W, MIX_W), F32)],
        compiler_params=_cp("arbitrary", "arbitrary"),
        name="hgrn2",
    )(z_c, z_c, *consts)


def _attn_kernel(tq, n_lat, zq_ref, zkv_ref, cos_ref, sin_ref, qg_ref, kg_ref, hones_ref, o_ref, k_s, v_s):
    j = pl.program_id(1)
    t = zkv_ref.shape[1]

    @pl.when(j == 0)
    def _():
        kf = zkv_ref[0][:, 256:384]
        kn = kf * lax.rsqrt(_headsum(kf * kf, hones_ref[...]) * (1.0 / HEAD_DIM) + RMS_EPS) * kg_ref[...]
        k_s[...] = _rope(kn, cos_ref[:, 0:128], sin_ref[:, 0:128]).astype(BF16)
        v_s[...] = zkv_ref[0][:, 384:512].astype(BF16)

    row0 = pl.multiple_of(j * tq, tq)
    qf = zq_ref[0][:, 0:256]
    qn = qf * lax.rsqrt(_headsum(qf * qf, hones_ref[...]) * (1.0 / HEAD_DIM) + RMS_EPS) * qg_ref[...]
    q = (_rope(qn, cos_ref[pl.ds(row0, tq), :], sin_ref[pl.ds(row0, tq), :])
         * (HEAD_DIM ** -0.5 * LOG2_E)).astype(BF16)

    def attend(key_lo, key_n):
        outs = []
        for h in range(MIX_HEADS):
            g = h // 2
            sc = lax.dot_general(q[:, h * 64:(h + 1) * 64], k_s[pl.ds(key_lo, key_n), g * 64:(g + 1) * 64],
                                 (((1,), (1,)), ((), ())), preferred_element_type=F32)
            e = jnp.exp2(sc - jnp.max(sc, axis=1, keepdims=True))
            pv = jnp.dot(e.astype(BF16), v_s[pl.ds(key_lo, key_n), g * 64:(g + 1) * 64], preferred_element_type=F32)
            outs.append(pv / jnp.sum(e, axis=1, keepdims=True))
        o_ref[0] = jnp.concatenate(outs, axis=1)

    @pl.when(row0 < n_lat)
    def _():
        attend(0, t)

    @pl.when(row0 >= n_lat)
    def _():
        attend(n_lat, t - n_lat)


def _attention_mixer(z_d, n_lat, tq, nq, cos, sin, q_g, k_g, hones):
    b, t, _ = z_d.shape
    return pl.pallas_call(
        functools.partial(_attn_kernel, tq, n_lat),
        grid=(b, nq),
        in_specs=[pl.BlockSpec((1, tq, COLS_D), lambda bi, j: (bi, j, 0)),
                  pl.BlockSpec((1, t, COLS_D), lambda bi, j: (bi, 0, 0)),
                  pl.BlockSpec((t, MIX_W), lambda bi, j: (0, 0)),
                  pl.BlockSpec((t, MIX_W), lambda bi, j: (0, 0)),
                  pl.BlockSpec((1, MIX_W), lambda bi, j: (0, 0)),
                  pl.BlockSpec((1, 128), lambda bi, j: (0, 0)),
                  _full_spec(hones)],
        out_specs=pl.BlockSpec((1, tq, MIX_W), lambda bi, j: (bi, j, 0)),
        out_shape=jax.ShapeDtypeStruct((b, nq * tq, MIX_W), F32),
        scratch_shapes=[pltpu.VMEM((t, 128), BF16), pltpu.VMEM((t, 128), BF16)],
        compiler_params=_cp("arbitrary", "arbitrary"),
        name="attention",
    )(z_d, z_d, cos, sin, q_g, k_g, hones)


def _layer_norm(x, g, b):
    mu = jnp.mean(x, axis=-1, keepdims=True)
    xc = x - mu
    var = jnp.mean(xc * xc, axis=-1, keepdims=True)
    return xc * lax.rsqrt(var + LN_EPS) * g + b


def _group_norm(y, eps, hones):
    mu = _headsum(y, hones) * (1.0 / HEAD_DIM)
    yc = y - mu
    var = _headsum(yc * yc, hones) * (1.0 / HEAD_DIM)
    return yc * lax.rsqrt(var + eps)


def _merge_kernel(alpha, n_lat_tiles, xl_ref, xc_ref, mod_ref, ry0, ry1, rbonus, rgate, ty0, ty1, tg, hy0, hy1, hg, at_ref,
                  rln_g, rln_b, tn_g, tn_b, hn_g, wg_ref, wb_ref, wo_ref, ln_g, ln_b, wr_ref, hones_ref,
                  x1_ref, u2_ref, aff_ref):
    d = xl_ref.shape[-1]
    mod = mod_ref[0, 0]
    x = jnp.where(pl.program_id(1) < n_lat_tiles, xl_ref[0], xc_ref[0])
    u = (x * (1.0 + mod[:, d:2 * d]) + mod[:, 0:d]).astype(BF16)
    hy = hy0[0] + hy1[0]
    hones = hones_ref[...]
    branches = (
        (_group_norm(ry0[0] + ry1[0], RWKV_GN_EPS, hones) * rln_g[...] + rln_b[...] + rbonus[0]) * rgate[0],
        (_group_norm(ty0[0] + ty1[0], LN_EPS, hones) * tn_g[...] + tn_b[...]) * _silu(tg[0]),
        hy * lax.rsqrt(_headsum(hy * hy, hones) * (1.0 / HEAD_DIM) + RMS_EPS) * hn_g[...] * _silu(hg[0]),
        at_ref[0],
    )
    merged = None
    for i, br in enumerate(branches):
        term = _sigmoid(jnp.dot(u, wg_ref[i], preferred_element_type=F32)) * _mm(br, wb_ref[i])
        merged = term if merged is None else merged + term
    mix = _mm(merged, wo_ref[...])
    x1 = _layer_norm(alpha * x + mod[:, 2 * d:3 * d] * mix, ln_g[...], ln_b[...])
    x1_ref[0] = x1
    u2 = (x1 * (1.0 + mod[:, 4 * d:5 * d]) + mod[:, 3 * d:4 * d]).astype(BF16)
    u2_ref[0] = u2
    logits = lax.dot_general(wr_ref[...], u2, (((1,), (1,)), ((), ())), preferred_element_type=F32)
    e = jnp.exp(logits - jnp.max(logits, axis=0, keepdims=True))
    aff_ref[0] = e / jnp.sum(e, axis=0, keepdims=True)


def _merge(alpha, x_lat, x_ctx, modsel, rw, rt, z_b, hg, z_c, att, prm, tm, n_lat_tiles, n_tiles):
    b, _, d = x_lat.shape
    rows = n_tiles * tm

    def tile(w):
        return pl.BlockSpec((1, tm, w), lambda i, j: (i, j, 0))

    def colblock(k):
        return pl.BlockSpec((1, tm, MIX_W), lambda i, j: (i, j, k))

    def full(a):
        return pl.BlockSpec(a.shape, lambda i, j, _n=a.ndim: (0,) * _n)

    params = [prm["rwkv_ln_g"], prm["rwkv_ln_b"], prm["ret_norm_g"], prm["ret_norm_b"], prm["hgrn_norm_g"],
              prm["w_gate"], prm["w_branch"], prm["w_out"], prm["ln1_g"], prm["ln1_b"], prm["w_router_t"],
              prm["hones"]]
    return pl.pallas_call(
        functools.partial(_merge_kernel, alpha, n_lat_tiles),
        grid=(b, n_tiles),
        in_specs=_two_stream_specs(tm, d, n_lat_tiles)
        + [pl.BlockSpec((1, 1, 1, 6 * d), lambda i, j: (i, (j >= n_lat_tiles).astype(jnp.int32), 0, 0))]
        + [tile(MIX_W)] * 4 + [tile(MIX_W)] * 2 + [colblock(3)] + [tile(MIX_W)] * 2 + [colblock(4)] + [tile(MIX_W)]
        + [full(a) for a in params],
        out_specs=[tile(d), tile(d), pl.BlockSpec((1, N_EXPERTS, tm), lambda i, j: (i, 0, j))],
        out_shape=[jax.ShapeDtypeStruct((b, rows, d), F32), jax.ShapeDtypeStruct((b, rows, d), BF16),
                   jax.ShapeDtypeStruct((b, N_EXPERTS, rows), F32)],
        compiler_params=_cp("arbitrary", "arbitrary"),
        name="merge",
    )(x_lat, x_ctx, modsel, *rw, rt[0], rt[1], z_b, hg[0], hg[1], z_c, att, *params)


def _route_kernel(start, n, cap, aff_ref, code_ref, gate_ref):
    a = aff_ref[0][:, start:start + n]
    bits = pltpu.bitcast(a, jnp.int32)

    def bisect(_, lohi):
        lo, hi = lohi
        mid = lo + ((hi - lo + 1) >> 1)
        ok = jnp.sum(jnp.where(bits >= mid, 1.0, 0.0), axis=1, keepdims=True) >= cap
        return jnp.where(ok, mid, lo), jnp.where(ok, hi, mid - 1)

    lo0 = jnp.zeros((N_EXPERTS, 1), jnp.int32)
    thr, _ = lax.fori_loop(0, 31, bisect, (lo0, jnp.full((N_EXPERTS, 1), 0x7F800000, jnp.int32)))
    gt, eq = bits > thr, bits == thr
    need = cap - jnp.sum(jnp.where(gt, 1.0, 0.0), axis=1, keepdims=True)
    blk = min(n, ROUTE_BLOCK)
    upper = jnp.where(_iota((blk, blk), 0) <= _iota((blk, blk), 1), 1.0, 0.0).astype(BF16)

    def prefix_count(m):
        parts, running = [], jnp.zeros((N_EXPERTS, 1), F32)
        for o in range(0, n, blk):
            pre = jnp.dot(jnp.where(m[:, o:o + blk], 1.0, 0.0).astype(BF16), upper, preferred_element_type=F32)
            parts.append(pre + running)
            running = running + pre[:, blk - 1:blk]
        return jnp.concatenate(parts, axis=1)

    sel = gt | (eq & (prefix_count(eq) <= need))
    rank = prefix_count(sel) - 1.0
    code_ref[0] = jnp.where(sel, rank.astype(jnp.int32), -1)
    gate_ref[0] = a


def _route(aff, start, n, cap):
    b, e, rows = aff.shape
    code, gate = pl.pallas_call(
        functools.partial(_route_kernel, start, n, cap),
        grid=(b,),
        in_specs=[pl.BlockSpec((1, e, rows), lambda i: (i, 0, 0))],
        out_specs=[pl.BlockSpec((1, e, n), lambda i: (i, 0, 0))] * 2,
        out_shape=[jax.ShapeDtypeStruct((b, e, n), jnp.int32), jax.ShapeDtypeStruct((b, e, n), F32)],
        compiler_params=_cp("arbitrary"),
        name="route",
    )(aff)
    return code.reshape(b, e, 1, n), gate.reshape(b, e, 1, n)


def _moe_kernel(bg, cap_pad, code_ref, gate_ref, u_ref, w1_ref, w3_ref, w2_ref, o_ref):
    @pl.when(pl.program_id(1) == 0)
    def _():
        o_ref[...] = jnp.zeros_like(o_ref)

    n = u_ref.shape[1]
    slot = _iota((cap_pad, n), 0)
    hits = [slot == code_ref[s, 0] for s in range(bg)]
    xs = jnp.concatenate([jnp.dot(jnp.where(h, 1.0, 0.0).astype(BF16), u_ref[s], preferred_element_type=F32)
                          for s, h in enumerate(hits)], axis=0).astype(BF16)
    hmid = _silu(jnp.dot(xs, w1_ref[0], preferred_element_type=F32)) * jnp.dot(xs, w3_ref[0],
                                                                               preferred_element_type=F32)
    y = jnp.dot(hmid.astype(BF16), w2_ref[0], preferred_element_type=F32)
    for s, h in enumerate(hits):
        weights = jnp.where(h, gate_ref[s, 0], 0.0)
        o_ref[s] += _mm_tn(weights, y[s * cap_pad:(s + 1) * cap_pad])


def _moe(code, gate, u2, row_block, n, cap, bg, w1, w3, w2):
    b = u2.shape[0]
    d = u2.shape[-1]
    cap_pad = max(cap, 128)
    f = w1.shape[-1]
    return pl.pallas_call(
        functools.partial(_moe_kernel, bg, cap_pad),
        grid=(b // bg, N_EXPERTS),
        in_specs=[pl.BlockSpec((bg, 1, 1, n), lambda i, e: (i, e, 0, 0)),
                  pl.BlockSpec((bg, 1, 1, n), lambda i, e: (i, e, 0, 0)),
                  pl.BlockSpec((bg, n, d), lambda i, e: (i, row_block, 0)),
                  pl.BlockSpec((1, d, f), lambda i, e: (e, 0, 0)),
                  pl.BlockSpec((1, d, f), lambda i, e: (e, 0, 0)),
                  pl.BlockSpec((1, f, d), lambda i, e: (e, 0, 0))],
        out_specs=pl.BlockSpec((bg, n, d), lambda i, e: (i, 0, 0)),
        out_shape=jax.ShapeDtypeStruct((b, n, d), F32),
        compiler_params=_cp("arbitrary", "arbitrary"),
        name="expert_ffn",
    )(code, gate, u2, w1, w3, w2)


def _cast_kernel(a_ref, b_ref, c_ref, oa_ref, ob_ref, oc_ref):
    oa_ref[0] = a_ref[0, 0].astype(BF16)
    ob_ref[0] = b_ref[0, 0].astype(BF16)
    oc_ref[0] = c_ref[0, 0].astype(BF16)


def _cast_expert_weights(w1, w3, w2, layer):
    e = w1.shape[1]
    src = lambda w: pl.BlockSpec((1, 1) + w.shape[2:], lambda i: (layer, i, 0, 0))
    dst = lambda w: pl.BlockSpec((1,) + w.shape[2:], lambda i: (i, 0, 0))
    return pl.pallas_call(
        _cast_kernel,
        grid=(e,),
        in_specs=[src(w1), src(w3), src(w2)],
        out_specs=[dst(w1), dst(w3), dst(w2)],
        out_shape=[jax.ShapeDtypeStruct(w.shape[1:], BF16) for w in (w1, w3, w2)],
        compiler_params=_cp("arbitrary"),
        name="cast_expert_weights",
    )(w1, w3, w2)


def _ln2_kernel(alpha, x_ref, f_ref, mod_ref, g_ref, b_ref, o_ref):
    d = x_ref.shape[-1]
    o_ref[0] = _layer_norm(alpha * x_ref[0] + mod_ref[0, 0][:, 5 * d:6 * d] * f_ref[0], g_ref[...], b_ref[...])


def _ln2(alpha, x1, x1_tile0, ffn, modsel, seg, g, bias, tm):
    b, rows, d = ffn.shape
    return pl.pallas_call(
        functools.partial(_ln2_kernel, alpha),
        grid=(b, rows // tm),
        in_specs=[pl.BlockSpec((1, tm, d), lambda i, j: (i, j + x1_tile0, 0)),
                  pl.BlockSpec((1, tm, d), lambda i, j: (i, j, 0)),
                  pl.BlockSpec((1, 1, 1, 6 * d), lambda i, j: (i, seg, 0, 0)),
                  pl.BlockSpec((1, d), lambda i, j: (0, 0)),
                  pl.BlockSpec((1, d), lambda i, j: (0, 0))],
        out_specs=pl.BlockSpec((1, tm, d), lambda i, j: (i, j, 0)),
        out_shape=jax.ShapeDtypeStruct((b, rows, d), F32),
        compiler_params=_cp("arbitrary", "arbitrary"),
        name="ln2",
    )(x1, ffn, modsel, g, bias)


def _rope_tables(n_lat, n_ctx):
    rows = n_lat // GRID_W
    row = jnp.repeat(jnp.arange(rows), GRID_W)
    col = jnp.tile(jnp.arange(GRID_W), rows)
    n_freq = HEAD_DIM // 4
    inv = ROPE_BASE ** (-jnp.arange(n_freq, dtype=F32) / n_freq)
    ang = jnp.concatenate([row[:, None] * inv, col[:, None] * inv], axis=-1)
    cos, sin = jnp.cos(ang), jnp.sin(ang)
    cos64 = jnp.concatenate([cos, cos], axis=-1)
    sin64 = jnp.concatenate([-sin, sin], axis=-1)
    cos64 = jnp.concatenate([cos64, jnp.ones((n_ctx, HEAD_DIM), F32)], axis=0)
    sin64 = jnp.concatenate([sin64, jnp.zeros((n_ctx, HEAD_DIM), F32)], axis=0)
    return jnp.tile(cos64, (1, MIX_HEADS)), jnp.tile(sin64, (1, MIX_HEADS))


def kernel(x, c, ctx, c_ctx, w_mod, b_mod, w_in, rwkv_mu, rwkv_w0, rwkv_w2, rwkv_a0, rwkv_a2, rwkv_g2, rwkv_kk, rwkv_ka, rwkv_rk, rwkv_ln_g, rwkv_ln_b, ret_decay, ret_norm_g, ret_norm_b, hgrn_lb, hgrn_norm_g, attn_q_g, attn_k_g, w_gate, w_branch, w_out, ln1_g, ln1_b, w_router, w_e1, w_e3, w_e2, ln2_g, ln2_b):
    bsz, n_lat, d = x.shape
    n_ctx = ctx.shape[1]
    depth = w_mod.shape[0]
    t = n_lat + n_ctx
    tm = min(256, n_ctx)
    assert n_lat % tm == 0 and n_ctx % tm == 0 and tm % CHUNK == 0 and n_lat % n_ctx == 0 and bsz % REC_BATCH == 0
    n_lat_tiles, n_tiles = n_lat // tm, t // tm
    alpha = (2 * depth) ** 0.25

    cos, sin = _rope_tables(n_lat, n_ctx)
    lb_w = jax.nn.softmax(hgrn_lb.astype(F32), axis=1)
    lower = jnp.cumsum(lb_w, axis=1) - lb_w[:, :1]
    log_lower = jnp.log(jnp.maximum(lower, HGRN_LB_FLOOR))
    log_gamma = jax.nn.log_sigmoid(ret_decay.astype(F32))
    cst = _scan_constants()

    rows = 8 * ((bsz + 1 + 7) // 8)
    cc = jnp.zeros((rows, d), F32).at[:bsz].set(c).at[bsz].set(c_ctx)
    mods = _modulation(cc, w_mod.astype(BF16), b_mod)

    x_lat, x_ctx = x, ctx
    row = lambda a: a.reshape(1, -1)
    for l in range(depth):
        need_ctx = l < depth - 1
        modsel = jnp.stack([mods[l, :bsz], jnp.broadcast_to(mods[l, bsz], (bsz, 6 * d))],
                           axis=1).reshape(bsz, 2, 1, 6 * d)
        z_a, z_b, z_c, z_d = _in_projection(x_lat, x_ctx, modsel, w_in[l].astype(BF16), tm, n_lat_tiles)
        rw = _rwkv_mixer(z_a, n_lat, {
            "mu": row(rwkv_mu[l]), "w0": rwkv_w0[l], "w2": rwkv_w2[l].astype(BF16), "a0": rwkv_a0[l],
            "a2": rwkv_a2[l].astype(BF16), "g2": rwkv_g2[l].astype(BF16), "k_k": row(rwkv_kk[l]),
            "k_a": row(rwkv_ka[l]), "r_k": row(rwkv_rk[l])}, cst)
        rt = _retention_mixer(z_b, n_lat, cos, sin, log_gamma[l], cst)
        hg = _hgrn_mixer(z_c, n_lat, lower[:, l], log_lower[:, l], cst)
        n_out_tiles = n_tiles if need_ctx else n_lat_tiles
        att = _attention_mixer(z_d, n_lat, tm, n_out_tiles, cos, sin,
                               row(jnp.tile(attn_q_g[l], MIX_HEADS)), row(jnp.tile(attn_k_g[l], 2)), cst["hones"])
        x1, u2, aff = _merge(alpha, x_lat, x_ctx, modsel, rw, rt, z_b, hg, z_c, att, {
            "rwkv_ln_g": row(rwkv_ln_g[l]), "rwkv_ln_b": row(rwkv_ln_b[l]), "ret_norm_g": row(ret_norm_g[l]),
            "ret_norm_b": row(ret_norm_b[l]), "hgrn_norm_g": row(hgrn_norm_g[l]),
            "w_gate": w_gate[l].astype(BF16), "w_branch": w_branch[l].astype(BF16), "w_out": w_out[l].astype(BF16),
            "ln1_g": row(ln1_g[l]), "ln1_b": row(ln1_b[l]), "w_router_t": w_router[l].T.astype(BF16),
            "hones": cst["hones"]},
            tm, n_lat_tiles, n_out_tiles)
        w1, w3, w2 = _cast_expert_weights(w_e1, w_e3, w_e2, l)
        cap_lat = EC_CAPACITY * n_lat // N_EXPERTS
        code, gate = _route(aff, 0, n_lat, cap_lat)
        ffn = _moe(code, gate, u2, 0, n_lat, cap_lat, 1, w1, w3, w2)
        x_lat = _ln2(alpha, x1, 0, ffn, modsel, 0, row(ln2_g[l]), row(ln2_b[l]), tm)
        if not need_ctx:
            return x_lat
        cap_ctx = EC_CAPACITY * n_ctx // N_EXPERTS
        code_c, gate_c = _route(aff, n_lat, n_ctx, cap_ctx)
        ffn_c = _moe(code_c, gate_c, u2, n_lat // n_ctx, n_ctx, cap_ctx, 8 if bsz % 8 == 0 else 1, w1, w3, w2)
        x_ctx = _ln2(alpha, x1, n_lat_tiles, ffn_c, modsel, 1, row(ln2_g[l]), row(ln2_b[l]), tm)
    return x_lat
```

```python
import functools

import jax
import jax.numpy as jnp
import numpy as np
from jax import lax
from jax.experimental import pallas as pl
from jax.experimental.pallas import tpu as pltpu

F32 = jnp.float32
BF16 = jnp.bfloat16

HEAD_DIM = 64
MIX_HEADS = 4
MIX_W = MIX_HEADS * HEAD_DIM
GRID_W = 64
ROPE_BASE = 10000.0
N_EXPERTS = 16
EC_CAPACITY = 2
LN_EPS = 1e-5
RMS_EPS = 1e-6
RWKV_GN_EPS = HEAD_DIM * 1e-5
HGRN_LB_FLOOR = 1e-20
LOG2_E = 1.4426950408889634
RWKV_LORA = 64
COLS_A, COLS_B, COLS_C, COLS_D = 1152, 1024, 1280, 512
CHUNK = 64
SUB_SHIFT = 3
SUB = 1 << SUB_SHIFT
EXP = MIX_HEADS * CHUNK
REC_BATCH = 8
ROUTE_BLOCK = 256
VMEM_LIMIT = 56 * 1024 * 1024


def _cp(*sem):
    return pltpu.CompilerParams(dimension_semantics=sem, vmem_limit_bytes=VMEM_LIMIT)


def _mm(a, b):
    return jnp.dot(a.astype(BF16), b.astype(BF16), preferred_element_type=F32)


def _mm_nt(a, b):
    return lax.dot_general(a.astype(BF16), b.astype(BF16), (((1,), (1,)), ((), ())), preferred_element_type=F32)


def _mm_tn(a, b):
    return lax.dot_general(a.astype(BF16), b.astype(BF16), (((0,), (0,)), ((), ())), preferred_element_type=F32)


def _split(x):
    hi = x.astype(BF16)
    lo = (x - hi.astype(F32)).astype(BF16)
    return hi, lo


def _mm_mask_l(mask_bf16, x):
    hi, lo = _split(x)
    return (jnp.dot(mask_bf16, hi, preferred_element_type=F32) + jnp.dot(mask_bf16, lo, preferred_element_type=F32))


def _mm_mask_r(x, mask_bf16):
    hi, lo = _split(x)
    return (jnp.dot(hi, mask_bf16, preferred_element_type=F32) + jnp.dot(lo, mask_bf16, preferred_element_type=F32))


def _iota(shape, dim):
    return lax.broadcasted_iota(jnp.int32, shape, dim)


def _headsum(x, hones):
    n = x.shape[-1]
    return _mm_mask_r(x, hones[0:n, 0:n])


def _sigmoid(x):
    return 1.0 / (1.0 + jnp.exp(-x))


def _silu(x):
    return x * _sigmoid(x)


def _softplus(x):
    return jnp.maximum(x, 0.0) + jnp.log(1.0 + jnp.exp(-jnp.abs(x)))


def _expand(x, headmask):
    tiled = jnp.concatenate([x.astype(BF16)] * MIX_HEADS, axis=0)
    return pltpu.bitcast(pltpu.bitcast(tiled, jnp.uint32) & headmask, BF16)


def _scan_constants():
    rows = np.arange(EXP)
    head = rows // CHUNK
    same = head[:, None] == head[None, :]
    lane_head = np.arange(MIX_W) // HEAD_DIM
    lane_tok = np.arange(MIX_W) % HEAD_DIM
    tri, cum, pick = [], [], []
    for dirn in (0, 1):
        flip = (lambda a: a) if dirn == 0 else (lambda a: CHUNK - 1 - a)
        p64, pl64 = flip(np.arange(CHUNK))[:, None], flip(lane_tok)[None, :]
        tri.append(np.stack([pl64 < p64, pl64 <= p64]))
        cum.append(p64.T <= p64)
        pick.append(np.stack([((p64 - pl64) == dd) & ((p64 >> SUB_SHIFT) == (pl64 >> SUB_SHIFT)) for dd in range(SUB)]))
    f = lambda a: jnp.asarray(np.asarray(a, np.float32))
    return {
        "headmask": jnp.asarray(np.where(head[::2, None] == lane_head[None, :], 0xFFFFFFFF, 0).astype(np.uint32)),
        "same": f(same),
        "tri": f(np.stack(tri)),
        "subcols": f((lane_tok[None, :] >> SUB_SHIFT) == np.arange(CHUNK // SUB)[:, None]),
        "cum": f(np.stack(cum)).astype(BF16),
        "pick": f(np.stack(pick)),
        "hones": f(lane_head[:, None] == lane_head[None, :]).astype(BF16),
    }


def _skewed(chains):
    results = [None] * len(chains)
    done, rnd = set(), 0
    while len(done) < len(chains):
        for idx in range(min(rnd + 1, len(chains))):
            if idx not in done:
                try:
                    next(chains[idx])
                except StopIteration as stop:
                    results[idx] = stop.value
                    done.add(idx)
        rnd += 1
    return results


def _full_spec(a):
    return pl.BlockSpec(a.shape, lambda *_, _n=a.ndim: (0,) * _n)


def _rope(x, cos, sin):
    blocks = [x[:, i:i + 128] for i in range(0, x.shape[-1], 128)]
    fwd = jnp.concatenate([pltpu.roll(b, 32, 1) for b in blocks], axis=1)
    bwd = jnp.concatenate([pltpu.roll(b, 128 - 32, 1) for b in blocks], axis=1)
    first = (_iota(x.shape, 1) & 63) < 32
    return x * cos + jnp.where(first, bwd, fwd) * sin


def _mod_kernel(c_ref, w_ref, b_ref, o_ref):
    o_ref[0] = _mm(_silu(c_ref[...]), w_ref[0]) + b_ref[0]


def _modulation(cc, w_mod, b_mod):
    depth, d, d6 = w_mod.shape
    tn = d6 // 4
    rows = cc.shape[0]
    return pl.pallas_call(
        _mod_kernel,
        grid=(depth, d6 // tn),
        in_specs=[pl.BlockSpec((rows, d), lambda l, j: (0, 0)),
                  pl.BlockSpec((1, d, tn), lambda l, j: (l, 0, j)),
                  pl.BlockSpec((1, 1, tn), lambda l, j: (l, 0, j))],
        out_specs=pl.BlockSpec((1, rows, tn), lambda l, j: (l, 0, j)),
        out_shape=jax.ShapeDtypeStruct((depth, rows, d6), F32),
        compiler_params=_cp("arbitrary", "arbitrary"),
        name="modulation",
    )(cc, w_mod, b_mod.reshape(depth, 1, d6))


def _inproj_kernel(n_lat_tiles, xl_ref, xc_ref, mod_ref, w_ref, za_ref, zb_ref, zc_ref, zd_ref):
    d = xl_ref.shape[-1]
    mod = mod_ref[0, 0]
    x = jnp.where(pl.program_id(1) < n_lat_tiles, xl_ref[0], xc_ref[0])
    u = (x * (1.0 + mod[:, d:2 * d]) + mod[:, 0:d]).astype(BF16)
    o = 0
    for ref, n in ((za_ref, COLS_A), (zb_ref, COLS_B), (zc_ref, COLS_C), (zd_ref, COLS_D)):
        ref[0] = jnp.dot(u, w_ref[:, o:o + n], preferred_element_type=F32)
        o += n


def _two_stream_specs(tm, d, n_lat_tiles):
    return [pl.BlockSpec((1, tm, d), lambda i, j: (i, jnp.minimum(j, n_lat_tiles - 1), 0)),
            pl.BlockSpec((1, tm, d), lambda i, j: (i, jnp.maximum(j - n_lat_tiles, 0), 0))]


def _in_projection(x_lat, x_ctx, modsel, w_in, tm, n_lat_tiles):
    b, n_lat, d = x_lat.shape
    t = n_lat + x_ctx.shape[1]
    d_in = w_in.shape[-1]
    cols = (COLS_A, COLS_B, COLS_C, COLS_D)
    return pl.pallas_call(
        functools.partial(_inproj_kernel, n_lat_tiles),
        grid=(b, t // tm),
        in_specs=_two_stream_specs(tm, d, n_lat_tiles) + [
            pl.BlockSpec((1, 1, 1, 6 * d), lambda i, j: (i, (j >= n_lat_tiles).astype(jnp.int32), 0, 0)),
            pl.BlockSpec((d, d_in), lambda i, j: (0, 0))],
        out_specs=[pl.BlockSpec((1, tm, n), lambda i, j: (i, j, 0)) for n in cols],
        out_shape=[jax.ShapeDtypeStruct((b, t, n), F32) for n in cols],
        compiler_params=_cp("arbitrary", "arbitrary"),
        name="in_projection",
    )(x_lat, x_ctx, modsel, w_in)


def _chunk_maps(n_lat_chunks, n_chunks):
    n_ctx_chunks = n_chunks - n_lat_chunks

    def fwd(i):
        return jnp.where(i < n_ctx_chunks, n_lat_chunks + i, i - n_ctx_chunks)

    def bwd(i):
        return jnp.where(i < n_ctx_chunks, n_chunks - 1 - i, n_chunks - 1 - i)

    return fwd, bwd


def _rwkv_direction(dirn, zc, prev_row, next_row, s_ref, p, cst):
    hm = cst["headmask"][...]
    rowi = _iota(zc.shape, 0)
    up = jnp.where(rowi == 0, prev_row, pltpu.roll(zc, 1, 0))
    dn = jnp.where(rowi == CHUNK - 1, next_row, pltpu.roll(zc, CHUNK - 1, 0))
    zs = zc + p["mu"] * (0.5 * (up + dn) - zc)
    yield
    r, k, v = zs[:, 0:256], zs[:, 256:512], zs[:, 512:768]
    zw = zs[:, 768 + RWKV_LORA * dirn:768 + RWKV_LORA * (dirn + 1)]
    za = zs[:, 896 + RWKV_LORA * dirn:896 + RWKV_LORA * (dirn + 1)]
    w = p["w0"][dirn:dirn + 1] + _mm(jnp.tanh(zw), p["w2"][dirn])
    lw = -jnp.exp(-_softplus(-w) - 0.5)
    yield
    a = _sigmoid(p["a0"][dirn:dirn + 1] + _mm(za, p["a2"][dirn]))
    kkf = k * p["k_k"]
    kk = kkf * lax.rsqrt(_headsum(kkf * kkf, cst["hones"][...]) + 1e-12)
    yield
    kd = k * (1.0 + (a - 1.0) * p["k_a"])
    bv = kk * a
    yield

    cum = _mm_mask_l(cst["cum"][dirn], lw)
    total = cum[CHUNK - 1:CHUNK] if dirn == 0 else cum[0:1]
    inv = jnp.exp(-cum)
    tail = jnp.exp(total - cum)
    yield
    lhs = jnp.concatenate([kk * jnp.exp(cum - lw), r * jnp.exp(cum)], axis=0)
    g_k = _mm_nt(lhs, _expand(kd * inv, hm))
    yield
    g_b = _mm_nt(lhs, _expand(bv * inv, hm))
    yield
    strict, incl = cst["tri"][dirn, 0], cst["tri"][dirn, 1]
    m_b = g_b[0:CHUNK] * strict
    n_b = g_b[CHUNK:2 * CHUNK] * incl
    mn_k = g_k * jnp.concatenate([strict, incl], axis=0)

    st = s_ref[...]
    carry = _mm_nt(lhs, st) + _mm(mn_k, _expand(v, hm))
    x = carry[0:CHUNK]
    yield
    x = x - _mm(m_b, _expand(x, hm))
    pw = _mm(m_b, _expand(m_b, hm))
    yield
    for step in range(5):
        x = x + _mm(pw, _expand(x, hm))
        if step < 4:
            pw = _mm(pw, _expand(pw, hm))
        yield
    y = carry[CHUNK:2 * CHUNK] - _mm(n_b, _expand(x, hm))
    yield
    upd = _mm_tn(jnp.concatenate([v, x], axis=0), jnp.concatenate([kd * tail, -(bv * tail)], axis=0))
    s_ref[...] = st * jnp.exp(total) + upd * cst["same"][...]
    return y, r, k, v, zs[:, 1024:1152]


def _rwkv_kernel(n_lat_chunks, n_chunks,
                 zf_ref, zfp_ref, zfn_ref, zb_ref, zbp_ref, zbn_ref,
                 mu_ref, w0_ref, w2_ref, a0_ref, a2_ref, g2_ref, kk_ref, ka_ref, rk_ref,
                 hm_ref, same_ref, tri_ref, cum_ref, hones_ref,
                 y0_ref, y1_ref, bonus_ref, gate_ref, s_ref):
    i = pl.program_id(1)
    fwd, bwd = _chunk_maps(n_lat_chunks, n_chunks)

    @pl.when(i == 0)
    def _():
        s_ref[...] = jnp.zeros_like(s_ref)

    p = {"mu": mu_ref[...], "w0": w0_ref[...], "w2": w2_ref, "a0": a0_ref[...], "a2": a2_ref,
         "k_k": kk_ref[...], "k_a": ka_ref[...]}
    cst = {"headmask": hm_ref, "same": same_ref, "tri": tri_ref, "cum": cum_ref, "hones": hones_ref}
    chains = []
    for s in range(REC_BATCH):
        for dirn, (z_ref, zp_ref, zn_ref) in enumerate(((zf_ref, zfp_ref, zfn_ref), (zb_ref, zbp_ref, zbn_ref))):
            c = fwd(i) if dirn == 0 else bwd(i)
            first = jnp.logical_or(c == 0, c == n_lat_chunks)
            last = jnp.logical_or(c == n_lat_chunks - 1, c == n_chunks - 1)
            prev_row = jnp.where(first, 0.0, zp_ref[s][7:8, :])
            next_row = jnp.where(last, 0.0, zn_ref[s][0:1, :])
            chains.append(_rwkv_direction(dirn, z_ref[s], prev_row, next_row, s_ref.at[s, dirn], p, cst))
    for idx, (y, r, k, v, zg) in enumerate(_skewed(chains)):
        s, dirn = divmod(idx, 2)
        if dirn == 0:
            y0_ref[s] = y
            bonus_ref[s] = _headsum(r * k * rk_ref[...], hones_ref[...]) * v
            gate_ref[s] = _mm(_sigmoid(zg), g2_ref[...])
        else:
            y1_ref[s] = y


def _rwkv_mixer(z_a, n_lat, prm, cst):
    b, t, _ = z_a.shape
    n_chunks, n_lat_chunks = t // CHUNK, n_lat // CHUNK
    fwd, bwd = _chunk_maps(n_lat_chunks, n_chunks)
    per8 = CHUNK // 8
    last8 = t // 8 - 1
    rb = REC_BATCH

    def zspecs(cm):
        return [pl.BlockSpec((rb, CHUNK, COLS_A), lambda bi, i: (bi, cm(i), 0)),
                pl.BlockSpec((rb, 8, COLS_A), lambda bi, i: (bi, jnp.maximum(cm(i) * per8 - 1, 0), 0)),
                pl.BlockSpec((rb, 8, COLS_A), lambda bi, i: (bi, jnp.minimum((cm(i) + 1) * per8, last8), 0))]

    params = [prm["mu"], prm["w0"], prm["w2"], prm["a0"], prm["a2"], prm["g2"], prm["k_k"], prm["k_a"], prm["r_k"],
              cst["headmask"], cst["same"], cst["tri"], cst["cum"], cst["hones"]]
    yspec_f = pl.BlockSpec((rb, CHUNK, MIX_W), lambda bi, i: (bi, fwd(i), 0))
    yspec_b = pl.BlockSpec((rb, CHUNK, MIX_W), lambda bi, i: (bi, bwd(i), 0))
    shape = jax.ShapeDtypeStruct((b, t, MIX_W), F32)
    return pl.pallas_call(
        functools.partial(_rwkv_kernel, n_lat_chunks, n_chunks),
        grid=(b // rb, n_chunks),
        in_specs=zspecs(fwd) + zspecs(bwd) + [_full_spec(a) for a in params],
        out_specs=[yspec_f, yspec_b, yspec_f, yspec_f],
        out_shape=[shape] * 4,
        scratch_shapes=[pltpu.VMEM((rb, 2, MIX_W, MIX_W), F32)],
        compiler_params=_cp("arbitrary", "arbitrary"),
        name="rwkv7",
    )(z_a, z_a, z_a, z_a, z_a, z_a, *params)


def _ret_direction(dirn, zc, cos, sin, lg, intra, chunk_decay, r_ref, hm, same):
    q = _rope(zc[:, 0:256], cos, sin)
    k = _rope(zc[:, 256:512], cos, sin) * HEAD_DIM ** -0.5
    v = zc[:, 512:768]
    pos = _iota((CHUNK, MIX_W), 0).astype(F32)
    if dirn == 1:
        pos = (CHUNK - 1.0) - pos
    yield
    sc = _mm_nt(q, _expand(k, hm)) * intra
    yield
    rs = r_ref[...]
    o = _mm(sc, _expand(v, hm)) + _mm(q * jnp.exp(lg * (pos + 1.0)), rs)
    yield
    r_ref[...] = rs * chunk_decay + _mm_tn(k * jnp.exp(lg * ((CHUNK - 1.0) - pos)), v) * same
    return o


def _ret_kernel(zf_ref, zb_ref, cf_ref, sf_ref, cb_ref, sb_ref, lg_ref, intra_ref, cd_ref, hm_ref, same_ref,
                y0_ref, y1_ref, r_ref):
    @pl.when(pl.program_id(1) == 0)
    def _():
        r_ref[...] = jnp.zeros_like(r_ref)

    hm, same = hm_ref[...], same_ref[...]
    chains = []
    for s in range(REC_BATCH):
        chains.append(_ret_direction(0, zf_ref[s], cf_ref[...], sf_ref[...], lg_ref[0:1], intra_ref[0], cd_ref[0],
                                     r_ref.at[s, 0], hm, same))
        chains.append(_ret_direction(1, zb_ref[s], cb_ref[...], sb_ref[...], lg_ref[1:2], intra_ref[1], cd_ref[1],
                                     r_ref.at[s, 1], hm, same))
    for idx, o in enumerate(_skewed(chains)):
        s, dirn = divmod(idx, 2)
        (y0_ref if dirn == 0 else y1_ref)[s] = o


def _retention_tables(log_gamma, cst):
    lane_tok = np.arange(MIX_W) % HEAD_DIM
    dist = np.abs(np.arange(CHUNK)[:, None] - lane_tok[None, :]).astype(np.float32)
    lg_lanes = jnp.repeat(log_gamma, HEAD_DIM, axis=-1)
    intra = jnp.exp(lg_lanes[:, None, :] * dist) * cst["tri"][:, 1]
    return intra, jnp.exp(lg_lanes[:, :, None] * float(CHUNK)) * cst["same"]


def _retention_mixer(z_b, n_lat, cos, sin, log_gamma, cst):
    b, t, _ = z_b.shape
    n_chunks, n_lat_chunks = t // CHUNK, n_lat // CHUNK
    fwd, bwd = _chunk_maps(n_lat_chunks, n_chunks)
    shape = jax.ShapeDtypeStruct((b, t, MIX_W), F32)
    rb = REC_BATCH
    intra, chunk_decay = _retention_tables(log_gamma, cst)
    lg_lanes = jnp.repeat(log_gamma, HEAD_DIM, axis=-1)

    def zs(cm):
        return pl.BlockSpec((rb, CHUNK, COLS_B), lambda bi, i: (bi, cm(i), 0))

    def ts(cm):
        return pl.BlockSpec((CHUNK, MIX_W), lambda bi, i: (cm(i), 0))

    consts = [lg_lanes, intra, chunk_decay, cst["headmask"], cst["same"]]
    return pl.pallas_call(
        _ret_kernel,
        grid=(b // rb, n_chunks),
        in_specs=[zs(fwd), zs(bwd), ts(fwd), ts(fwd), ts(bwd), ts(bwd)] + [_full_spec(a) for a in consts],
        out_specs=[pl.BlockSpec((rb, CHUNK, MIX_W), lambda bi, i: (bi, fwd(i), 0)),
                   pl.BlockSpec((rb, CHUNK, MIX_W), lambda bi, i: (bi, bwd(i), 0))],
        out_shape=[shape] * 2,
        scratch_shapes=[pltpu.VMEM((rb, 2, MIX_W, MIX_W), F32)],
        compiler_params=_cp("arbitrary", "arbitrary"),
        name="retention",
    )(z_b, z_b, cos, sin, cos, sin, *consts)


def _hgrn_direction(dirn, zc, log_keep, log_lb, s_ref, cst):
    hm = cst["headmask"][...]
    qs = _silu(zc[:, 0:256])
    fz = zc[:, 256 + MIX_W * dirn:256 + MIX_W * (dirn + 1)]
    v = zc[:, 768:1024]
    tail_term = jnp.log(1.0 + jnp.exp(-jnp.abs(fz)))
    ls_pos = -(jnp.maximum(-fz, 0.0) + tail_term)
    ls_neg = -(jnp.maximum(fz, 0.0) + tail_term)
    p1, p2 = ls_pos, log_lb + ls_neg
    log_f = jnp.maximum(p1, p2) + jnp.log(1.0 + jnp.exp(-jnp.abs(p1 - p2)))
    yield
    bc = _mm_mask_l(cst["cum"][dirn], log_f * LOG2_E)
    total = bc[CHUNK - 1:CHUNK] if dirn == 0 else bc[0:1]
    excl = bc - log_f * LOG2_E
    lk = (log_keep + ls_neg) * LOG2_E - bc

    nsub = CHUNK // SUB
    first_tok = [b * SUB if dirn == 0 else b * SUB + SUB - 1 for b in range(nsub)]
    last_tok = [b * SUB + SUB - 1 if dirn == 0 else b * SUB for b in range(nsub)]
    e_start = [excl[i:i + 1] for i in first_tok]
    e_end = [bc[i:i + 1] for i in last_tok]
    row_ref = jnp.concatenate([jnp.broadcast_to(e, (SUB, MIX_W)) for e in e_start], axis=0)
    key_ref = jnp.concatenate([jnp.broadcast_to(e, (SUB, MIX_W)) for e in e_end], axis=0)
    q_rel = qs * jnp.exp2(bc - row_ref)
    k_rel = jnp.exp2(key_ref + lk)
    yield
    pairs = [(bi, bj) for bi in range(nsub) for bj in range(nsub) if (bj < bi if dirn == 0 else bj > bi)]
    lhs = jnp.concatenate([q_rel[bi * SUB:(bi + 1) * SUB] * jnp.exp2(e_start[bi] - e_end[bj]) for bi, bj in pairs],
                          axis=0)
    g = _mm_nt(lhs, _expand(k_rel, hm))
    yield
    acc = [None] * nsub
    for idx, (bi, bj) in enumerate(pairs):
        part = g[idx * SUB:(idx + 1) * SUB] * cst["subcols"][bj:bj + 1]
        acc[bi] = part if acc[bi] is None else acc[bi] + part
    att = jnp.concatenate([a if a is not None else jnp.zeros((SUB, MIX_W), F32) for a in acc], axis=0)
    yield
    prods = []
    for dd in range(SUB):
        sh = dd if dirn == 0 else (CHUNK - dd) % CHUNK
        lr = lk if dd == 0 else pltpu.roll(lk, sh, 0)
        prods.append((qs * jnp.exp2(jnp.minimum(bc + lr, 0.0))).astype(BF16))
        if dd % 2 == 1:
            yield
    val = jnp.dot(jnp.concatenate(prods, axis=0), cst["hones"][...], preferred_element_type=F32)
    yield
    for dd in range(SUB):
        att = att + val[dd * CHUNK:(dd + 1) * CHUNK] * cst["pick"][dirn, dd]

    st = s_ref[...]
    yield
    o = _mm(att, _expand(v, hm)) + _mm_nt(qs * jnp.exp2(bc), st)
    yield
    s_ref[...] = st * jnp.exp2(total) + _mm_tn(v, jnp.exp2(total + lk)) * cst["same"][...]
    return o


def _hgrn_kernel(zf_ref, zb_ref, keep_ref, llb_ref, hm_ref, same_ref, cum_ref, subcols_ref, pick_ref, hones_ref,
                 y0_ref, y1_ref, s_ref):
    @pl.when(pl.program_id(1) == 0)
    def _():
        s_ref[...] = jnp.zeros_like(s_ref)

    cst = {"headmask": hm_ref, "same": same_ref, "cum": cum_ref, "subcols": subcols_ref, "pick": pick_ref,
           "hones": hones_ref}
    chains = []
    for s in range(REC_BATCH):
        chains.append(_hgrn_direction(0, zf_ref[s], keep_ref[0:1], llb_ref[0:1], s_ref.at[s, 0], cst))
        chains.append(_hgrn_direction(1, zb_ref[s], keep_ref[1:2], llb_ref[1:2], s_ref.at[s, 1], cst))
    for idx, o in enumerate(_skewed(chains)):
        s, dirn = divmod(idx, 2)
        (y0_ref if dirn == 0 else y1_ref)[s] = o


def _hgrn_mixer(z_c, n_lat, log_keep, log_lb, cst):
    b, t, _ = z_c.shape
    n_chunks, n_lat_chunks = t // CHUNK, n_lat // CHUNK
    fwd, bwd = _chunk_maps(n_lat_chunks, n_chunks)
    shape = jax.ShapeDtypeStruct((b, t, MIX_W), F32)
    rb = REC_BATCH
    consts = [log_keep, log_lb, cst["headmask"], cst["same"], cst["cum"], cst["subcols"], cst["pick"], cst["hones"]]
    return pl.pallas_call(
        _hgrn_kernel,
        grid=(b // rb, n_chunks),
        in_specs=[pl.BlockSpec((rb, CHUNK, COLS_C), lambda bi, i: (bi, fwd(i), 0)),
                  pl.BlockSpec((rb, CHUNK, COLS_C), lambda bi, i: (bi, bwd(i), 0))] + [_full_spec(a) for a in consts],
        out_specs=[pl.BlockSpec((rb, CHUNK, MIX_W), lambda bi, i: (bi, fwd(i), 0)),
                   pl.BlockSpec((rb, CHUNK, MIX_W), lambda bi, i: (bi, bwd(i), 0))],
        out_shape=[shape] * 2,
        scratch_shapes=[pltpu.VMEM((rb, 2, MIX_W, MIX_W), F32)],
        compiler_params=_cp("arbitrary", "arbitrary"),
        name="hgrn2",
    )(z_c, z_c, *consts)


def _attn_kernel(tq, n_lat, zq_ref, zkv_ref, cos_ref, sin_ref, qg_ref, kg_ref, hones_ref, o_ref, k_s, v_s):
    j = pl.program_id(1)
    t = zkv_ref.shape[1]

    @pl.when(j == 0)
    def _():
        kf = zkv_ref[0][:, 256:384]
        kn = kf * lax.rsqrt(_headsum(kf * kf, hones_ref[...]) * (1.0 / HEAD_DIM) + RMS_EPS) * kg_ref[...]
        k_s[...] = _rope(kn, cos_ref[:, 0:128], sin_ref[:, 0:128]).astype(BF16)
        v_s[...] = zkv_ref[0][:, 384:512].astype(BF16)

    row0 = pl.multiple_of(j * tq, tq)
    qf = zq_ref[0][:, 0:256]
    qn = qf * lax.rsqrt(_headsum(qf * qf, hones_ref[...]) * (1.0 / HEAD_DIM) + RMS_EPS) * qg_ref[...]
    q = (_rope(qn, cos_ref[pl.ds(row0, tq), :], sin_ref[pl.ds(row0, tq), :])
         * (HEAD_DIM ** -0.5 * LOG2_E)).astype(BF16)

    def head(h, key_lo, key_n):
        g = h // 2
        sc = lax.dot_general(q[:, h * 64:(h + 1) * 64], k_s[pl.ds(key_lo, key_n), g * 64:(g + 1) * 64],
                             (((1,), (1,)), ((), ())), preferred_element_type=F32)
        yield
        e = jnp.exp2(sc - jnp.max(sc, axis=1, keepdims=True))
        den = jnp.sum(e, axis=1, keepdims=True)
        e = e.astype(BF16)
        yield
        pv = jnp.dot(e, v_s[pl.ds(key_lo, key_n), g * 64:(g + 1) * 64], preferred_element_type=F32)
        return pv / den

    def attend(key_lo, key_n):
        o_ref[0] = jnp.concatenate(_skewed([head(h, key_lo, key_n) for h in range(MIX_HEADS)]), axis=1)

    @pl.when(row0 < n_lat)
    def _():
        attend(0, t)

    @pl.when(row0 >= n_lat)
    def _():
        attend(n_lat, t - n_lat)


def _attention_mixer(z_d, n_lat, tq, nq, cos, sin, q_g, k_g, hones):
    b, t, _ = z_d.shape
    return pl.pallas_call(
        functools.partial(_attn_kernel, tq, n_lat),
        grid=(b, nq),
        in_specs=[pl.BlockSpec((1, tq, COLS_D), lambda bi, j: (bi, j, 0)),
                  pl.BlockSpec((1, t, COLS_D), lambda bi, j: (bi, 0, 0)),
                  pl.BlockSpec((t, MIX_W), lambda bi, j: (0, 0)),
                  pl.BlockSpec((t, MIX_W), lambda bi, j: (0, 0)),
                  pl.BlockSpec((1, MIX_W), lambda bi, j: (0, 0)),
                  pl.BlockSpec((1, 128), lambda bi, j: (0, 0)),
                  _full_spec(hones)],
        out_specs=pl.BlockSpec((1, tq, MIX_W), lambda bi, j: (bi, j, 0)),
        out_shape=jax.ShapeDtypeStruct((b, nq * tq, MIX_W), F32),
        scratch_shapes=[pltpu.VMEM((t, 128), BF16), pltpu.VMEM((t, 128), BF16)],
        compiler_params=_cp("arbitrary", "arbitrary"),
        name="attention",
    )(z_d, z_d, cos, sin, q_g, k_g, hones)


def _layer_norm(x, g, b):
    mu = jnp.mean(x, axis=-1, keepdims=True)
    xc = x - mu
    var = jnp.mean(xc * xc, axis=-1, keepdims=True)
    return xc * lax.rsqrt(var + LN_EPS) * g + b


def _group_norm(y, eps, hones):
    mu = _headsum(y, hones) * (1.0 / HEAD_DIM)
    yc = y - mu
    var = _headsum(yc * yc, hones) * (1.0 / HEAD_DIM)
    return yc * lax.rsqrt(var + eps)


def _merge_kernel(alpha, n_lat_tiles, xl_ref, xc_ref, mod_ref, ry0, ry1, rbonus, rgate, ty0, ty1, tg, hy0, hy1, hg, at_ref,
                  rln_g, rln_b, tn_g, tn_b, hn_g, wg_ref, wb_ref, wo_ref, ln_g, ln_b, wr_ref, hones_ref,
                  x1_ref, u2_ref, aff_ref):
    d = xl_ref.shape[-1]
    mod = mod_ref[0, 0]
    x = jnp.where(pl.program_id(1) < n_lat_tiles, xl_ref[0], xc_ref[0])
    u = (x * (1.0 + mod[:, d:2 * d]) + mod[:, 0:d]).astype(BF16)
    hy = hy0[0] + hy1[0]
    hones = hones_ref[...]
    branches = (
        (_group_norm(ry0[0] + ry1[0], RWKV_GN_EPS, hones) * rln_g[...] + rln_b[...] + rbonus[0]) * rgate[0],
        (_group_norm(ty0[0] + ty1[0], LN_EPS, hones) * tn_g[...] + tn_b[...]) * _silu(tg[0]),
        hy * lax.rsqrt(_headsum(hy * hy, hones) * (1.0 / HEAD_DIM) + RMS_EPS) * hn_g[...] * _silu(hg[0]),
        at_ref[0],
    )
    merged = None
    for i, br in enumerate(branches):
        term = _sigmoid(jnp.dot(u, wg_ref[i], preferred_element_type=F32)) * _mm(br, wb_ref[i])
        merged = term if merged is None else merged + term
    mix = _mm(merged, wo_ref[...])
    x1 = _layer_norm(alpha * x + mod[:, 2 * d:3 * d] * mix, ln_g[...], ln_b[...])
    x1_ref[0] = x1
    u2 = (x1 * (1.0 + mod[:, 4 * d:5 * d]) + mod[:, 3 * d:4 * d]).astype(BF16)
    u2_ref[0] = u2
    logits = lax.dot_general(wr_ref[...], u2, (((1,), (1,)), ((), ())), preferred_element_type=F32)
    e = jnp.exp(logits - jnp.max(logits, axis=0, keepdims=True))
    aff_ref[0] = e / jnp.sum(e, axis=0, keepdims=True)


def _merge(alpha, x_lat, x_ctx, modsel, rw, rt, z_b, hg, z_c, att, prm, tm, n_lat_tiles, n_tiles):
    b, _, d = x_lat.shape
    rows = n_tiles * tm

    def tile(w):
        return pl.BlockSpec((1, tm, w), lambda i, j: (i, j, 0))

    def colblock(k):
        return pl.BlockSpec((1, tm, MIX_W), lambda i, j: (i, j, k))

    def full(a):
        return pl.BlockSpec(a.shape, lambda i, j, _n=a.ndim: (0,) * _n)

    params = [prm["rwkv_ln_g"], prm["rwkv_ln_b"], prm["ret_norm_g"], prm["ret_norm_b"], prm["hgrn_norm_g"],
              prm["w_gate"], prm["w_branch"], prm["w_out"], prm["ln1_g"], prm["ln1_b"], prm["w_router_t"],
              prm["hones"]]
    return pl.pallas_call(
        functools.partial(_merge_kernel, alpha, n_lat_tiles),
        grid=(b, n_tiles),
        in_specs=_two_stream_specs(tm, d, n_lat_tiles)
        + [pl.BlockSpec((1, 1, 1, 6 * d), lambda i, j: (i, (j >= n_lat_tiles).astype(jnp.int32), 0, 0))]
        + [tile(MIX_W)] * 4 + [tile(MIX_W)] * 2 + [colblock(3)] + [tile(MIX_W)] * 2 + [colblock(4)] + [tile(MIX_W)]
        + [full(a) for a in params],
        out_specs=[tile(d), tile(d), pl.BlockSpec((1, N_EXPERTS, tm), lambda i, j: (i, 0, j))],
        out_shape=[jax.ShapeDtypeStruct((b, rows, d), F32), jax.ShapeDtypeStruct((b, rows, d), BF16),
                   jax.ShapeDtypeStruct((b, N_EXPERTS, rows), F32)],
        compiler_params=_cp("arbitrary", "arbitrary"),
        name="merge",
    )(x_lat, x_ctx, modsel, *rw, rt[0], rt[1], z_b, hg[0], hg[1], z_c, att, *params)


def _route_kernel(start, n, cap, aff_ref, code_ref, gate_ref):
    a = aff_ref[0][:, start:start + n]
    bits = pltpu.bitcast(a, jnp.int32)

    def bisect(_, lohi):
        lo, hi = lohi
        mid = lo + ((hi - lo + 1) >> 1)
        ok = jnp.sum(jnp.where(bits >= mid, 1.0, 0.0), axis=1, keepdims=True) >= cap
        return jnp.where(ok, mid, lo), jnp.where(ok, hi, mid - 1)

    lo0 = jnp.zeros((N_EXPERTS, 1), jnp.int32)
    thr, _ = lax.fori_loop(0, 31, bisect, (lo0, jnp.full((N_EXPERTS, 1), 0x7F800000, jnp.int32)))
    gt, eq = bits > thr, bits == thr
    need = cap - jnp.sum(jnp.where(gt, 1.0, 0.0), axis=1, keepdims=True)
    blk = min(n, ROUTE_BLOCK)
    upper = jnp.where(_iota((blk, blk), 0) <= _iota((blk, blk), 1), 1.0, 0.0).astype(BF16)

    def prefix_count(m):
        parts, running = [], jnp.zeros((N_EXPERTS, 1), F32)
        for o in range(0, n, blk):
            pre = jnp.dot(jnp.where(m[:, o:o + blk], 1.0, 0.0).astype(BF16), upper, preferred_element_type=F32)
            parts.append(pre + running)
            running = running + pre[:, blk - 1:blk]
        return jnp.concatenate(parts, axis=1)

    sel = gt | (eq & (prefix_count(eq) <= need))
    rank = prefix_count(sel) - 1.0
    code_ref[0] = jnp.where(sel, rank.astype(jnp.int32), -1)
    gate_ref[0] = a


def _route(aff, start, n, cap):
    b, e, rows = aff.shape
    code, gate = pl.pallas_call(
        functools.partial(_route_kernel, start, n, cap),
        grid=(b,),
        in_specs=[pl.BlockSpec((1, e, rows), lambda i: (i, 0, 0))],
        out_specs=[pl.BlockSpec((1, e, n), lambda i: (i, 0, 0))] * 2,
        out_shape=[jax.ShapeDtypeStruct((b, e, n), jnp.int32), jax.ShapeDtypeStruct((b, e, n), F32)],
        compiler_params=_cp("arbitrary"),
        name="route",
    )(aff)
    return code.reshape(b, e, 1, n), gate.reshape(b, e, 1, n)


def _moe_kernel(bg, cap_pad, code_ref, gate_ref, u_ref, w1_ref, w3_ref, w2_ref, o_ref):
    @pl.when(pl.program_id(1) == 0)
    def _():
        o_ref[...] = jnp.zeros_like(o_ref)

    n = u_ref.shape[1]
    slot = _iota((cap_pad, n), 0)
    hits = [slot == code_ref[s, 0] for s in range(bg)]
    xs = jnp.concatenate([jnp.dot(jnp.where(h, 1.0, 0.0).astype(BF16), u_ref[s], preferred_element_type=F32)
                          for s, h in enumerate(hits)], axis=0).astype(BF16)
    hmid = _silu(jnp.dot(xs, w1_ref[0], preferred_element_type=F32)) * jnp.dot(xs, w3_ref[0],
                                                                               preferred_element_type=F32)
    y = jnp.dot(hmid.astype(BF16), w2_ref[0], preferred_element_type=F32)
    for s, h in enumerate(hits):
        weights = jnp.where(h, gate_ref[s, 0], 0.0)
        o_ref[s] += _mm_tn(weights, y[s * cap_pad:(s + 1) * cap_pad])


def _moe(code, gate, u2, row_block, n, cap, bg, w1, w3, w2):
    b = u2.shape[0]
    d = u2.shape[-1]
    cap_pad = cap
    f = w1.shape[-1]
    return pl.pallas_call(
        functools.partial(_moe_kernel, bg, cap_pad),
        grid=(b // bg, N_EXPERTS),
        in_specs=[pl.BlockSpec((bg, 1, 1, n), lambda i, e: (i, e, 0, 0)),
                  pl.BlockSpec((bg, 1, 1, n), lambda i, e: (i, e, 0, 0)),
                  pl.BlockSpec((bg, n, d), lambda i, e: (i, row_block, 0)),
                  pl.BlockSpec((1, d, f), lambda i, e: (e, 0, 0)),
                  pl.BlockSpec((1, d, f), lambda i, e: (e, 0, 0)),
                  pl.BlockSpec((1, f, d), lambda i, e: (e, 0, 0))],
        out_specs=pl.BlockSpec((bg, n, d), lambda i, e: (i, 0, 0)),
        out_shape=jax.ShapeDtypeStruct((b, n, d), F32),
        compiler_params=_cp("arbitrary", "arbitrary"),
        name="expert_ffn",
    )(code, gate, u2, w1, w3, w2)


def _cast_kernel(a_ref, b_ref, c_ref, oa_ref, ob_ref, oc_ref):
    oa_ref[0] = a_ref[0, 0].astype(BF16)
    ob_ref[0] = b_ref[0, 0].astype(BF16)
    oc_ref[0] = c_ref[0, 0].astype(BF16)


def _cast_expert_weights(w1, w3, w2, layer):
    e = w1.shape[1]
    src = lambda w: pl.BlockSpec((1, 1) + w.shape[2:], lambda i: (layer, i, 0, 0))
    dst = lambda w: pl.BlockSpec((1,) + w.shape[2:], lambda i: (i, 0, 0))
    return pl.pallas_call(
        _cast_kernel,
        grid=(e,),
        in_specs=[src(w1), src(w3), src(w2)],
        out_specs=[dst(w1), dst(w3), dst(w2)],
        out_shape=[jax.ShapeDtypeStruct(w.shape[1:], BF16) for w in (w1, w3, w2)],
        compiler_params=_cp("arbitrary"),
        name="cast_expert_weights",
    )(w1, w3, w2)


def _ln2_kernel(alpha, x_ref, f_ref, mod_ref, g_ref, b_ref, o_ref):
    d = x_ref.shape[-1]
    o_ref[0] = _layer_norm(alpha * x_ref[0] + mod_ref[0, 0][:, 5 * d:6 * d] * f_ref[0], g_ref[...], b_ref[...])


def _ln2(alpha, x1, x1_tile0, ffn, modsel, seg, g, bias, tm):
    b, rows, d = ffn.shape
    return pl.pallas_call(
        functools.partial(_ln2_kernel, alpha),
        grid=(b, rows // tm),
        in_specs=[pl.BlockSpec((1, tm, d), lambda i, j: (i, j + x1_tile0, 0)),
                  pl.BlockSpec((1, tm, d), lambda i, j: (i, j, 0)),
                  pl.BlockSpec((1, 1, 1, 6 * d), lambda i, j: (i, seg, 0, 0)),
                  pl.BlockSpec((1, d), lambda i, j: (0, 0)),
                  pl.BlockSpec((1, d), lambda i, j: (0, 0))],
        out_specs=pl.BlockSpec((1, tm, d), lambda i, j: (i, j, 0)),
        out_shape=jax.ShapeDtypeStruct((b, rows, d), F32),
        compiler_params=_cp("arbitrary", "arbitrary"),
        name="ln2",
    )(x1, ffn, modsel, g, bias)


def _rope_tables(n_lat, n_ctx):
    rows = n_lat // GRID_W
    row = jnp.repeat(jnp.arange(rows), GRID_W)
    col = jnp.tile(jnp.arange(GRID_W), rows)
    n_freq = HEAD_DIM // 4
    inv = ROPE_BASE ** (-jnp.arange(n_freq, dtype=F32) / n_freq)
    ang = jnp.concatenate([row[:, None] * inv, col[:, None] * inv], axis=-1)
    cos, sin = jnp.cos(ang), jnp.sin(ang)
    cos64 = jnp.concatenate([cos, cos], axis=-1)
    sin64 = jnp.concatenate([-sin, sin], axis=-1)
    cos64 = jnp.concatenate([cos64, jnp.ones((n_ctx, HEAD_DIM), F32)], axis=0)
    sin64 = jnp.concatenate([sin64, jnp.zeros((n_ctx, HEAD_DIM), F32)], axis=0)
    return jnp.tile(cos64, (1, MIX_HEADS)), jnp.tile(sin64, (1, MIX_HEADS))


def kernel(x, c, ctx, c_ctx, w_mod, b_mod, w_in, rwkv_mu, rwkv_w0, rwkv_w2, rwkv_a0, rwkv_a2, rwkv_g2, rwkv_kk, rwkv_ka, rwkv_rk, rwkv_ln_g, rwkv_ln_b, ret_decay, ret_norm_g, ret_norm_b, hgrn_lb, hgrn_norm_g, attn_q_g, attn_k_g, w_gate, w_branch, w_out, ln1_g, ln1_b, w_router, w_e1, w_e3, w_e2, ln2_g, ln2_b):
    bsz, n_lat, d = x.shape
    n_ctx = ctx.shape[1]
    depth = w_mod.shape[0]
    t = n_lat + n_ctx
    tm = min(256, n_ctx)
    assert n_lat % tm == 0 and n_ctx % tm == 0 and tm % CHUNK == 0 and n_lat % n_ctx == 0 and bsz % REC_BATCH == 0
    n_lat_tiles, n_tiles = n_lat // tm, t // tm
    alpha = (2 * depth) ** 0.25

    cos, sin = _rope_tables(n_lat, n_ctx)
    lb_w = jax.nn.softmax(hgrn_lb.astype(F32), axis=1)
    lower = jnp.cumsum(lb_w, axis=1) - lb_w[:, :1]
    log_lower = jnp.log(jnp.maximum(lower, HGRN_LB_FLOOR))
    log_gamma = jax.nn.log_sigmoid(ret_decay.astype(F32))
    cst = _scan_constants()

    rows = 8 * ((bsz + 1 + 7) // 8)
    cc = jnp.zeros((rows, d), F32).at[:bsz].set(c).at[bsz].set(c_ctx)
    mods = _modulation(cc, w_mod.astype(BF16), b_mod)

    x_lat, x_ctx = x, ctx
    row = lambda a: a.reshape(1, -1)
    for l in range(depth):
        need_ctx = l < depth - 1
        modsel = jnp.stack([mods[l, :bsz], jnp.broadcast_to(mods[l, bsz], (bsz, 6 * d))],
                           axis=1).reshape(bsz, 2, 1, 6 * d)
        z_a, z_b, z_c, z_d = _in_projection(x_lat, x_ctx, modsel, w_in[l].astype(BF16), tm, n_lat_tiles)
        rw = _rwkv_mixer(z_a, n_lat, {
            "mu": row(rwkv_mu[l]), "w0": rwkv_w0[l], "w2": rwkv_w2[l].astype(BF16), "a0": rwkv_a0[l],
            "a2": rwkv_a2[l].astype(BF16), "g2": rwkv_g2[l].astype(BF16), "k_k": row(rwkv_kk[l]),
            "k_a": row(rwkv_ka[l]), "r_k": row(rwkv_rk[l])}, cst)
        rt = _retention_mixer(z_b, n_lat, cos, sin, log_gamma[l], cst)
        hg = _hgrn_mixer(z_c, n_lat, jnp.log(1.0 - lower[:, l]), log_lower[:, l], cst)
        n_out_tiles = n_tiles if need_ctx else n_lat_tiles
        att = _attention_mixer(z_d, n_lat, tm, n_out_tiles, cos, sin,
                               row(jnp.tile(attn_q_g[l], MIX_HEADS)), row(jnp.tile(attn_k_g[l], 2)), cst["hones"])
        x1, u2, aff = _merge(alpha, x_lat, x_ctx, modsel, rw, rt, z_b, hg, z_c, att, {
            "rwkv_ln_g": row(rwkv_ln_g[l]), "rwkv_ln_b": row(rwkv_ln_b[l]), "ret_norm_g": row(ret_norm_g[l]),
            "ret_norm_b": row(ret_norm_b[l]), "hgrn_norm_g": row(hgrn_norm_g[l]),
            "w_gate": w_gate[l].astype(BF16), "w_branch": w_branch[l].astype(BF16), "w_out": w_out[l].astype(BF16),
            "ln1_g": row(ln1_g[l]), "ln1_b": row(ln1_b[l]), "w_router_t": w_router[l].T.astype(BF16),
            "hones": cst["hones"]},
            tm, n_lat_tiles, n_out_tiles)
        w1, w3, w2 = _cast_expert_weights(w_e1, w_e3, w_e2, l)
        cap_lat = EC_CAPACITY * n_lat // N_EXPERTS
        code, gate = _route(aff, 0, n_lat, cap_lat)
        ffn = _moe(code, gate, u2, 0, n_lat, cap_lat, 1, w1, w3, w2)
        x_lat = _ln2(alpha, x1, 0, ffn, modsel, 0, row(ln2_g[l]), row(ln2_b[l]), tm)
        if not need_ctx:
            return x_lat
        cap_ctx = EC_CAPACITY * n_ctx // N_EXPERTS
        code_c, gate_c = _route(aff, n_lat, n_ctx, cap_ctx)
        ffn_c = _moe(code_c, gate_c, u2, n_lat // n_ctx, n_ctx, cap_ctx, 8 if bsz % 8 == 0 else 1, w1, w3, w2)
        x_ctx = _ln2(alpha, x1, n_lat_tiles, ffn_c, modsel, 1, row(ln2_g[l]), row(ln2_b[l]), tm)
    return x_lat
```

```python
import functools

import jax
import jax.numpy as jnp
import numpy as np
from jax import lax
from jax.experimental import pallas as pl
from jax.experimental.pallas import tpu as pltpu

F32 = jnp.float32
BF16 = jnp.bfloat16

HEAD_DIM = 64
MIX_HEADS = 4
MIX_W = MIX_HEADS * HEAD_DIM
KV_HEADS = 2
KV_W = KV_HEADS * HEAD_DIM
LANES = 128
GRID_W = 64
ROPE_BASE = 10000.0
N_EXPERTS = 16
EC_CAPACITY = 2
LN_EPS = 1e-5
RMS_EPS = 1e-6
RWKV_GN_EPS = HEAD_DIM * 1e-5
HGRN_LB_FLOOR = 1e-20
LOG2_E = 1.4426950408889634
RWKV_LORA = 64
RWKV_LORA_G = 128
COLS_A = 3 * MIX_W + 4 * RWKV_LORA + RWKV_LORA_G
COLS_B = 4 * MIX_W
COLS_C = 5 * MIX_W
COLS_D = MIX_W + 2 * KV_W
CHUNK = 64
SUB_SHIFT = 3
SUB = 1 << SUB_SHIFT
EXP = MIX_HEADS * CHUNK
REC_BATCH = 16
ROUTE_BLOCK = 256
LN_ROWS = 512
VMEM_LIMIT = 56 * 1024 * 1024


def _cp(*sem):
    return pltpu.CompilerParams(dimension_semantics=sem, vmem_limit_bytes=VMEM_LIMIT)


def _mm(a, b):
    return jnp.dot(a.astype(BF16), b.astype(BF16), preferred_element_type=F32)


def _mm_nt(a, b):
    return lax.dot_general(a.astype(BF16), b.astype(BF16), (((1,), (1,)), ((), ())), preferred_element_type=F32)


def _mm_tn(a, b):
    return lax.dot_general(a.astype(BF16), b.astype(BF16), (((0,), (0,)), ((), ())), preferred_element_type=F32)


def _split(x):
    hi = x.astype(BF16)
    lo = (x - hi.astype(F32)).astype(BF16)
    return hi, lo


def _mm_mask_l(mask_bf16, x):
    hi, lo = _split(x)
    return (jnp.dot(mask_bf16, hi, preferred_element_type=F32) + jnp.dot(mask_bf16, lo, preferred_element_type=F32))


def _mm_mask_r(x, mask_bf16):
    hi, lo = _split(x)
    return (jnp.dot(hi, mask_bf16, preferred_element_type=F32) + jnp.dot(lo, mask_bf16, preferred_element_type=F32))


def _iota(shape, dim):
    return lax.broadcasted_iota(jnp.int32, shape, dim)


def _headsum(x, hones):
    n = x.shape[-1]
    return _mm_mask_r(x, hones[0:n, 0:n])


def _sigmoid(x):
    return 1.0 / (1.0 + jnp.exp(-x))


def _silu(x):
    return x * _sigmoid(x)


def _softplus(x):
    return jnp.maximum(x, 0.0) + jnp.log(1.0 + jnp.exp(-jnp.abs(x)))


def _expand(x, headmask):
    tiled = jnp.concatenate([x.astype(BF16)] * MIX_HEADS, axis=0)
    return pltpu.bitcast(pltpu.bitcast(tiled, jnp.uint32) & headmask, BF16)


def _scan_constants():
    rows = np.arange(EXP)
    head = rows // CHUNK
    same = head[:, None] == head[None, :]
    lane_head = np.arange(MIX_W) // HEAD_DIM
    lane_tok = np.arange(MIX_W) % HEAD_DIM
    tri, cum, pick = [], [], []
    for dirn in (0, 1):
        flip = (lambda a: a) if dirn == 0 else (lambda a: CHUNK - 1 - a)
        p64, pl64 = flip(np.arange(CHUNK))[:, None], flip(lane_tok)[None, :]
        tri.append(np.stack([pl64 < p64, pl64 <= p64]))
        cum.append(p64.T <= p64)
        pick.append(np.stack([((p64 - pl64) == dd) & ((p64 >> SUB_SHIFT) == (pl64 >> SUB_SHIFT)) for dd in range(SUB)]))
    f = lambda a: jnp.asarray(np.asarray(a, np.float32))
    return {
        "headmask": jnp.asarray(np.where(head[::2, None] == lane_head[None, :], 0xFFFFFFFF, 0).astype(np.uint32)),
        "same": f(same),
        "tri": f(np.stack(tri)),
        "subcols": f((lane_tok[None, :] >> SUB_SHIFT) == np.arange(CHUNK // SUB)[:, None]),
        "cum": f(np.stack(cum)).astype(BF16),
        "pick": f(np.stack(pick)),
        "hones": f(lane_head[:, None] == lane_head[None, :]).astype(BF16),
    }


def _skewed(chains):
    results = [None] * len(chains)
    done, rnd = set(), 0
    while len(done) < len(chains):
        for idx in range(min(rnd + 1, len(chains))):
            if idx not in done:
                try:
                    next(chains[idx])
                except StopIteration as stop:
                    results[idx] = stop.value
                    done.add(idx)
        rnd += 1
    return results


def _full_spec(a):
    return pl.BlockSpec(a.shape, lambda *_, _n=a.ndim: (0,) * _n)


def _rope(x, cos, sin):
    half = HEAD_DIM // 2
    blocks = [x[:, i:i + LANES] for i in range(0, x.shape[-1], LANES)]
    fwd = jnp.concatenate([pltpu.roll(b, half, 1) for b in blocks], axis=1)
    bwd = jnp.concatenate([pltpu.roll(b, LANES - half, 1) for b in blocks], axis=1)
    first = (_iota(x.shape, 1) & (HEAD_DIM - 1)) < half
    return x * cos + jnp.where(first, bwd, fwd) * sin


def _mod_kernel(c_ref, w_ref, b_ref, o_ref):
    o_ref[0] = _mm(_silu(c_ref[...]), w_ref[0]) + b_ref[0]


def _modulation(cc, w_mod, b_mod):
    depth, d, d6 = w_mod.shape
    tn = d6 // 4
    rows = cc.shape[0]
    return pl.pallas_call(
        _mod_kernel,
        grid=(depth, d6 // tn),
        in_specs=[pl.BlockSpec((rows, d), lambda l, j: (0, 0)),
                  pl.BlockSpec((1, d, tn), lambda l, j: (l, 0, j)),
                  pl.BlockSpec((1, 1, tn), lambda l, j: (l, 0, j))],
        out_specs=pl.BlockSpec((1, rows, tn), lambda l, j: (l, 0, j)),
        out_shape=jax.ShapeDtypeStruct((depth, rows, d6), F32),
        compiler_params=_cp("arbitrary", "arbitrary"),
        name="modulation",
    )(cc, w_mod, b_mod.reshape(depth, 1, d6))


def _inproj_kernel(n_lat_tiles, xl_ref, xc_ref, mod_ref, w_ref, za_ref, zb_ref, zc_ref, zd_ref):
    d = xl_ref.shape[-1]
    mod = mod_ref[0, 0]
    x = jnp.where(pl.program_id(1) < n_lat_tiles, xl_ref[0], xc_ref[0])
    u = (x * (1.0 + mod[:, d:2 * d]) + mod[:, 0:d]).astype(BF16)
    o = 0
    for ref, n in ((za_ref, COLS_A), (zb_ref, COLS_B), (zc_ref, COLS_C), (zd_ref, COLS_D)):
        ref[0] = jnp.dot(u, w_ref[:, o:o + n], preferred_element_type=F32)
        o += n


def _two_stream_specs(tm, d, n_lat_tiles):
    return [pl.BlockSpec((1, tm, d), lambda i, j: (i, jnp.minimum(j, n_lat_tiles - 1), 0)),
            pl.BlockSpec((1, tm, d), lambda i, j: (i, jnp.maximum(j - n_lat_tiles, 0), 0))]


def _in_projection(x_lat, x_ctx, modsel, w_in, tm, n_lat_tiles):
    b, n_lat, d = x_lat.shape
    t = n_lat + x_ctx.shape[1]
    d_in = w_in.shape[-1]
    cols = (COLS_A, COLS_B, COLS_C, COLS_D)
    return pl.pallas_call(
        functools.partial(_inproj_kernel, n_lat_tiles),
        grid=(b, t // tm),
        in_specs=_two_stream_specs(tm, d, n_lat_tiles) + [
            pl.BlockSpec((1, 1, 1, 6 * d), lambda i, j: (i, (j >= n_lat_tiles).astype(jnp.int32), 0, 0)),
            pl.BlockSpec((d, d_in), lambda i, j: (0, 0))],
        out_specs=[pl.BlockSpec((1, tm, n), lambda i, j: (i, j, 0)) for n in cols],
        out_shape=[jax.ShapeDtypeStruct((b, t, n), F32) for n in cols],
        compiler_params=_cp("arbitrary", "arbitrary"),
        name="in_projection",
    )(x_lat, x_ctx, modsel, w_in)


def _chunk_maps(n_lat_chunks, n_chunks):
    n_ctx_chunks = n_chunks - n_lat_chunks

    def fwd(i):
        return jnp.where(i < n_ctx_chunks, n_lat_chunks + i, i - n_ctx_chunks)

    def bwd(i):
        return jnp.where(i < n_ctx_chunks, n_chunks - 1 - i, n_chunks - 1 - i)

    return fwd, bwd


def _rwkv_direction(dirn, zc, prev_row, next_row, s_ref, p, cst):
    hm = cst["headmask"][...]
    rowi = _iota(zc.shape, 0)
    up = jnp.where(rowi == 0, prev_row, pltpu.roll(zc, 1, 0))
    dn = jnp.where(rowi == CHUNK - 1, next_row, pltpu.roll(zc, CHUNK - 1, 0))
    zs = zc + p["mu"] * (0.5 * (up + dn) - zc)
    yield
    r, k, v = zs[:, 0:MIX_W], zs[:, MIX_W:2 * MIX_W], zs[:, 2 * MIX_W:3 * MIX_W]
    w_lo = 3 * MIX_W + RWKV_LORA * dirn
    a_lo = 3 * MIX_W + RWKV_LORA * (2 + dirn)
    zw, za = zs[:, w_lo:w_lo + RWKV_LORA], zs[:, a_lo:a_lo + RWKV_LORA]
    w = p["w0"][dirn:dirn + 1] + _mm(jnp.tanh(zw), p["w2"][dirn])
    lw = -jnp.exp(-_softplus(-w) - 0.5)
    yield
    a = _sigmoid(p["a0"][dirn:dirn + 1] + _mm(za, p["a2"][dirn]))
    kkf = k * p["k_k"]
    kk = kkf * lax.rsqrt(_headsum(kkf * kkf, cst["hones"][...]) + 1e-12)
    yield
    kd = k * (1.0 + (a - 1.0) * p["k_a"])
    bv = kk * a
    yield

    cum = _mm_mask_l(cst["cum"][dirn], lw)
    total = cum[CHUNK - 1:CHUNK] if dirn == 0 else cum[0:1]
    inv = jnp.exp(-cum)
    tail = jnp.exp(total - cum)
    yield
    lhs = jnp.concatenate([kk * jnp.exp(cum - lw), r * jnp.exp(cum)], axis=0)
    g_k = _mm_nt(lhs, _expand(kd * inv, hm))
    yield
    g_b = _mm_nt(lhs, _expand(bv * inv, hm))
    yield
    strict, incl = cst["tri"][dirn, 0], cst["tri"][dirn, 1]
    m_b = g_b[0:CHUNK] * strict
    n_b = g_b[CHUNK:2 * CHUNK] * incl
    mn_k = g_k * jnp.concatenate([strict, incl], axis=0)

    st = s_ref[...]
    carry = _mm_nt(lhs, st) + _mm(mn_k, _expand(v, hm))
    x = carry[0:CHUNK]
    yield
    x = x - _mm(m_b, _expand(x, hm))
    pw = _mm(m_b, _expand(m_b, hm))
    yield
    for step in range(5):
        x = x + _mm(pw, _expand(x, hm))
        if step < 4:
            pw = _mm(pw, _expand(pw, hm))
        yield
    y = carry[CHUNK:2 * CHUNK] - _mm(n_b, _expand(x, hm))
    yield
    upd = _mm_tn(jnp.concatenate([v, x], axis=0), jnp.concatenate([kd * tail, -(bv * tail)], axis=0))
    s_ref[...] = st * jnp.exp(total) + upd * cst["same"][...]
    return y, r, k, v, zs[:, COLS_A - RWKV_LORA_G:COLS_A]


def _rwkv_kernel(n_lat_chunks, n_chunks,
                 zf_ref, zfp_ref, zfn_ref, zb_ref, zbp_ref, zbn_ref,
                 mu_ref, w0_ref, w2_ref, a0_ref, a2_ref, g2_ref, kk_ref, ka_ref, rk_ref,
                 hm_ref, same_ref, tri_ref, cum_ref, hones_ref,
                 y0_ref, y1_ref, bonus_ref, gate_ref, s_ref):
    i = pl.program_id(1)
    fwd, bwd = _chunk_maps(n_lat_chunks, n_chunks)

    @pl.when(i == 0)
    def _():
        s_ref[...] = jnp.zeros_like(s_ref)

    p = {"mu": mu_ref[...], "w0": w0_ref[...], "w2": w2_ref, "a0": a0_ref[...], "a2": a2_ref,
         "k_k": kk_ref[...], "k_a": ka_ref[...]}
    cst = {"headmask": hm_ref, "same": same_ref, "tri": tri_ref, "cum": cum_ref, "hones": hones_ref}
    chains = []
    for s in range(REC_BATCH):
        for dirn, (z_ref, zp_ref, zn_ref) in enumerate(((zf_ref, zfp_ref, zfn_ref), (zb_ref, zbp_ref, zbn_ref))):
            c = fwd(i) if dirn == 0 else bwd(i)
            first = jnp.logical_or(c == 0, c == n_lat_chunks)
            last = jnp.logical_or(c == n_lat_chunks - 1, c == n_chunks - 1)
            prev_row = jnp.where(first, 0.0, zp_ref[s][7:8, :])
            next_row = jnp.where(last, 0.0, zn_ref[s][0:1, :])
            chains.append(_rwkv_direction(dirn, z_ref[s], prev_row, next_row, s_ref.at[s, dirn], p, cst))
    for idx, (y, r, k, v, zg) in enumerate(_skewed(chains)):
        s, dirn = divmod(idx, 2)
        if dirn == 0:
            y0_ref[s] = y
            bonus_ref[s] = _headsum(r * k * rk_ref[...], hones_ref[...]) * v
            gate_ref[s] = _mm(_sigmoid(zg), g2_ref[...])
        else:
            y1_ref[s] = y


def _rwkv_mixer(z_a, n_lat, prm, cst):
    b, t, _ = z_a.shape
    n_chunks, n_lat_chunks = t // CHUNK, n_lat // CHUNK
    fwd, bwd = _chunk_maps(n_lat_chunks, n_chunks)
    per8 = CHUNK // 8
    last8 = t // 8 - 1
    rb = REC_BATCH

    def zspecs(cm):
        return [pl.BlockSpec((rb, CHUNK, COLS_A), lambda bi, i: (bi, cm(i), 0)),
                pl.BlockSpec((rb, 8, COLS_A), lambda bi, i: (bi, jnp.maximum(cm(i) * per8 - 1, 0), 0)),
                pl.BlockSpec((rb, 8, COLS_A), lambda bi, i: (bi, jnp.minimum((cm(i) + 1) * per8, last8), 0))]

    params = [prm["mu"], prm["w0"], prm["w2"], prm["a0"], prm["a2"], prm["g2"], prm["k_k"], prm["k_a"], prm["r_k"],
              cst["headmask"], cst["same"], cst["tri"], cst["cum"], cst["hones"]]
    yspec_f = pl.BlockSpec((rb, CHUNK, MIX_W), lambda bi, i: (bi, fwd(i), 0))
    yspec_b = pl.BlockSpec((rb, CHUNK, MIX_W), lambda bi, i: (bi, bwd(i), 0))
    shape = jax.ShapeDtypeStruct((b, t, MIX_W), F32)
    return pl.pallas_call(
        functools.partial(_rwkv_kernel, n_lat_chunks, n_chunks),
        grid=(b // rb, n_chunks),
        in_specs=zspecs(fwd) + zspecs(bwd) + [_full_spec(a) for a in params],
        out_specs=[yspec_f, yspec_b, yspec_f, yspec_f],
        out_shape=[shape] * 4,
        scratch_shapes=[pltpu.VMEM((rb, 2, MIX_W, MIX_W), F32)],
        compiler_params=_cp("arbitrary", "arbitrary"),
        name="rwkv7",
    )(z_a, z_a, z_a, z_a, z_a, z_a, *params)


def _ret_direction(dirn, zc, cos, sin, lg, intra, chunk_decay, r_ref, hm, same):
    q = _rope(zc[:, 0:MIX_W], cos, sin)
    k = _rope(zc[:, MIX_W:2 * MIX_W], cos, sin) * HEAD_DIM ** -0.5
    v = zc[:, 2 * MIX_W:3 * MIX_W]
    pos = _iota((CHUNK, MIX_W), 0).astype(F32)
    if dirn == 1:
        pos = (CHUNK - 1.0) - pos
    yield
    sc = _mm_nt(q, _expand(k, hm)) * intra
    yield
    rs = r_ref[...]
    o = _mm(sc, _expand(v, hm)) + _mm(q * jnp.exp(lg * (pos + 1.0)), rs)
    yield
    r_ref[...] = rs * chunk_decay + _mm_tn(k * jnp.exp(lg * ((CHUNK - 1.0) - pos)), v) * same
    return o


def _ret_kernel(zf_ref, zb_ref, cf_ref, sf_ref, cb_ref, sb_ref, lg_ref, intra_ref, cd_ref, hm_ref, same_ref,
                y0_ref, y1_ref, r_ref):
    @pl.when(pl.program_id(1) == 0)
    def _():
        r_ref[...] = jnp.zeros_like(r_ref)

    hm, same = hm_ref[...], same_ref[...]
    chains = []
    for s in range(REC_BATCH):
        chains.append(_ret_direction(0, zf_ref[s], cf_ref[...], sf_ref[...], lg_ref[0:1], intra_ref[0], cd_ref[0],
                                     r_ref.at[s, 0], hm, same))
        chains.append(_ret_direction(1, zb_ref[s], cb_ref[...], sb_ref[...], lg_ref[1:2], intra_ref[1], cd_ref[1],
                                     r_ref.at[s, 1], hm, same))
    for idx, o in enumerate(_skewed(chains)):
        s, dirn = divmod(idx, 2)
        (y0_ref if dirn == 0 else y1_ref)[s] = o


def _retention_tables(log_gamma, cst):
    lane_tok = np.arange(MIX_W) % HEAD_DIM
    dist = np.abs(np.arange(CHUNK)[:, None] - lane_tok[None, :]).astype(np.float32)
    lg_lanes = jnp.repeat(log_gamma, HEAD_DIM, axis=-1)
    intra = jnp.exp(lg_lanes[:, None, :] * dist) * cst["tri"][:, 1]
    return intra, jnp.exp(lg_lanes[:, :, None] * float(CHUNK)) * cst["same"]


def _retention_mixer(z_b, n_lat, cos, sin, log_gamma, cst):
    b, t, _ = z_b.shape
    n_chunks, n_lat_chunks = t // CHUNK, n_lat // CHUNK
    fwd, bwd = _chunk_maps(n_lat_chunks, n_chunks)
    shape = jax.ShapeDtypeStruct((b, t, MIX_W), F32)
    rb = REC_BATCH
    intra, chunk_decay = _retention_tables(log_gamma, cst)
    lg_lanes = jnp.repeat(log_gamma, HEAD_DIM, axis=-1)

    def zs(cm):
        return pl.BlockSpec((rb, CHUNK, COLS_B), lambda bi, i: (bi, cm(i), 0))

    def ts(cm):
        return pl.BlockSpec((CHUNK, MIX_W), lambda bi, i: (cm(i), 0))

    consts = [lg_lanes, intra, chunk_decay, cst["headmask"], cst["same"]]
    return pl.pallas_call(
        _ret_kernel,
        grid=(b // rb, n_chunks),
        in_specs=[zs(fwd), zs(bwd), ts(fwd), ts(fwd), ts(bwd), ts(bwd)] + [_full_spec(a) for a in consts],
        out_specs=[pl.BlockSpec((rb, CHUNK, MIX_W), lambda bi, i: (bi, fwd(i), 0)),
                   pl.BlockSpec((rb, CHUNK, MIX_W), lambda bi, i: (bi, bwd(i), 0))],
        out_shape=[shape] * 2,
        scratch_shapes=[pltpu.VMEM((rb, 2, MIX_W, MIX_W), F32)],
        compiler_params=_cp("arbitrary", "arbitrary"),
        name="retention",
    )(z_b, z_b, cos, sin, cos, sin, *consts)


def _hgrn_direction(dirn, zc, log_keep, log_lb, s_ref, cst):
    hm = cst["headmask"][...]
    qs = _silu(zc[:, 0:MIX_W])
    fz = zc[:, MIX_W * (1 + dirn):MIX_W * (2 + dirn)]
    v = zc[:, 3 * MIX_W:4 * MIX_W]
    tail_term = jnp.log(1.0 + jnp.exp(-jnp.abs(fz)))
    ls_pos = -(jnp.maximum(-fz, 0.0) + tail_term)
    ls_neg = -(jnp.maximum(fz, 0.0) + tail_term)
    p1, p2 = ls_pos, log_lb + ls_neg
    log_f = jnp.maximum(p1, p2) + jnp.log(1.0 + jnp.exp(-jnp.abs(p1 - p2)))
    yield
    bc = _mm_mask_l(cst["cum"][dirn], log_f * LOG2_E)
    total = bc[CHUNK - 1:CHUNK] if dirn == 0 else bc[0:1]
    excl = bc - log_f * LOG2_E
    lk = (log_keep + ls_neg) * LOG2_E - bc

    nsub = CHUNK // SUB
    first_tok = [b * SUB if dirn == 0 else b * SUB + SUB - 1 for b in range(nsub)]
    last_tok = [b * SUB + SUB - 1 if dirn == 0 else b * SUB for b in range(nsub)]
    e_start = [excl[i:i + 1] for i in first_tok]
    e_end = [bc[i:i + 1] for i in last_tok]
    row_ref = jnp.concatenate([jnp.broadcast_to(e, (SUB, MIX_W)) for e in e_start], axis=0)
    key_ref = jnp.concatenate([jnp.broadcast_to(e, (SUB, MIX_W)) for e in e_end], axis=0)
    q_rel = qs * jnp.exp2(bc - row_ref)
    k_rel = jnp.exp2(key_ref + lk)
    yield
    pairs = [(bi, bj) for bi in range(nsub) for bj in range(nsub) if (bj < bi if dirn == 0 else bj > bi)]
    lhs = jnp.concatenate([q_rel[bi * SUB:(bi + 1) * SUB] * jnp.exp2(e_start[bi] - e_end[bj]) for bi, bj in pairs],
                          axis=0)
    g = _mm_nt(lhs, _expand(k_rel, hm))
    yield
    acc = [None] * nsub
    for idx, (bi, bj) in enumerate(pairs):
        part = g[idx * SUB:(idx + 1) * SUB] * cst["subcols"][bj:bj + 1]
        acc[bi] = part if acc[bi] is None else acc[bi] + part
    att = jnp.concatenate([a if a is not None else jnp.zeros((SUB, MIX_W), F32) for a in acc], axis=0)
    yield
    prods = []
    for dd in range(SUB):
        sh = dd if dirn == 0 else (CHUNK - dd) % CHUNK
        lr = lk if dd == 0 else pltpu.roll(lk, sh, 0)
        prods.append((qs * jnp.exp2(jnp.minimum(bc + lr, 0.0))).astype(BF16))
        if dd % 2 == 1:
            yield
    val = jnp.dot(jnp.concatenate(prods, axis=0), cst["hones"][...], preferred_element_type=F32)
    yield
    for dd in range(SUB):
        att = att + val[dd * CHUNK:(dd + 1) * CHUNK] * cst["pick"][dirn, dd]

    st = s_ref[...]
    yield
    o = _mm(att, _expand(v, hm)) + _mm_nt(qs * jnp.exp2(bc), st)
    yield
    s_ref[...] = st * jnp.exp2(total) + _mm_tn(v, jnp.exp2(total + lk)) * cst["same"][...]
    return o


def _hgrn_kernel(zf_ref, zb_ref, keep_ref, llb_ref, hm_ref, same_ref, cum_ref, subcols_ref, pick_ref, hones_ref,
                 y0_ref, y1_ref, s_ref):
    @pl.when(pl.program_id(1) == 0)
    def _():
        s_ref[...] = jnp.zeros_like(s_ref)

    cst = {"headmask": hm_ref, "same": same_ref, "cum": cum_ref, "subcols": subcols_ref, "pick": pick_ref,
           "hones": hones_ref}
    chains = []
    for s in range(REC_BATCH):
        chains.append(_hgrn_direction(0, zf_ref[s], keep_ref[0:1], llb_ref[0:1], s_ref.at[s, 0], cst))
        chains.append(_hgrn_direction(1, zb_ref[s], keep_ref[1:2], llb_ref[1:2], s_ref.at[s, 1], cst))
    for idx, o in enumerate(_skewed(chains)):
        s, dirn = divmod(idx, 2)
        (y0_ref if dirn == 0 else y1_ref)[s] = o


def _hgrn_mixer(z_c, n_lat, log_keep, log_lb, cst):
    b, t, _ = z_c.shape
    n_chunks, n_lat_chunks = t // CHUNK, n_lat // CHUNK
    fwd, bwd = _chunk_maps(n_lat_chunks, n_chunks)
    shape = jax.ShapeDtypeStruct((b, t, MIX_W), F32)
    rb = REC_BATCH
    consts = [log_keep, log_lb, cst["headmask"], cst["same"], cst["cum"], cst["subcols"], cst["pick"], cst["hones"]]
    return pl.pallas_call(
        _hgrn_kernel,
        grid=(b // rb, n_chunks),
        in_specs=[pl.BlockSpec((rb, CHUNK, COLS_C), lambda bi, i: (bi, fwd(i), 0)),
                  pl.BlockSpec((rb, CHUNK, COLS_C), lambda bi, i: (bi, bwd(i), 0))] + [_full_spec(a) for a in consts],
        out_specs=[pl.BlockSpec((rb, CHUNK, MIX_W), lambda bi, i: (bi, fwd(i), 0)),
                   pl.BlockSpec((rb, CHUNK, MIX_W), lambda bi, i: (bi, bwd(i), 0))],
        out_shape=[shape] * 2,
        scratch_shapes=[pltpu.VMEM((rb, 2, MIX_W, MIX_W), F32)],
        compiler_params=_cp("arbitrary", "arbitrary"),
        name="hgrn2",
    )(z_c, z_c, *consts)


def _attn_kernel(tq, n_lat, zq_ref, zkv_ref, cos_ref, sin_ref, qg_ref, kg_ref, hones_ref, o_ref, k_s, v_s):
    j = pl.program_id(1)
    t = zkv_ref.shape[1]

    @pl.when(j == 0)
    def _():
        kf = zkv_ref[0][:, MIX_W:MIX_W + KV_W]
        kn = kf * lax.rsqrt(_headsum(kf * kf, hones_ref[...]) * (1.0 / HEAD_DIM) + RMS_EPS) * kg_ref[...]
        k_s[...] = _rope(kn, cos_ref[:, 0:KV_W], sin_ref[:, 0:KV_W]).astype(BF16)
        v_s[...] = zkv_ref[0][:, MIX_W + KV_W:COLS_D].astype(BF16)

    row0 = pl.multiple_of(j * tq, tq)
    qf = zq_ref[0][:, 0:MIX_W]
    qn = qf * lax.rsqrt(_headsum(qf * qf, hones_ref[...]) * (1.0 / HEAD_DIM) + RMS_EPS) * qg_ref[...]
    q = (_rope(qn, cos_ref[pl.ds(row0, tq), :], sin_ref[pl.ds(row0, tq), :])
         * (HEAD_DIM ** -0.5 * LOG2_E)).astype(BF16)

    def head(h, key_lo, key_n):
        g = h // (MIX_HEADS // KV_HEADS)
        hd = HEAD_DIM
        sc = lax.dot_general(q[:, h * hd:(h + 1) * hd], k_s[pl.ds(key_lo, key_n), g * hd:(g + 1) * hd],
                             (((1,), (1,)), ((), ())), preferred_element_type=F32)
        yield
        e = jnp.exp2(sc - jnp.max(sc, axis=1, keepdims=True))
        den = jnp.sum(e, axis=1, keepdims=True)
        e = e.astype(BF16)
        yield
        pv = jnp.dot(e, v_s[pl.ds(key_lo, key_n), g * hd:(g + 1) * hd], preferred_element_type=F32)
        return pv / den

    def attend(key_lo, key_n):
        o_ref[0] = jnp.concatenate(_skewed([head(h, key_lo, key_n) for h in range(MIX_HEADS)]), axis=1)

    @pl.when(row0 < n_lat)
    def _():
        attend(0, t)

    @pl.when(row0 >= n_lat)
    def _():
        attend(n_lat, t - n_lat)


def _attention_mixer(z_d, n_lat, tq, nq, cos, sin, q_g, k_g, hones):
    b, t, _ = z_d.shape
    return pl.pallas_call(
        functools.partial(_attn_kernel, tq, n_lat),
        grid=(b, nq),
        in_specs=[pl.BlockSpec((1, tq, COLS_D), lambda bi, j: (bi, j, 0)),
                  pl.BlockSpec((1, t, COLS_D), lambda bi, j: (bi, 0, 0)),
                  pl.BlockSpec((t, MIX_W), lambda bi, j: (0, 0)),
                  pl.BlockSpec((t, MIX_W), lambda bi, j: (0, 0)),
                  pl.BlockSpec((1, MIX_W), lambda bi, j: (0, 0)),
                  pl.BlockSpec((1, KV_W), lambda bi, j: (0, 0)),
                  _full_spec(hones)],
        out_specs=pl.BlockSpec((1, tq, MIX_W), lambda bi, j: (bi, j, 0)),
        out_shape=jax.ShapeDtypeStruct((b, nq * tq, MIX_W), F32),
        scratch_shapes=[pltpu.VMEM((t, KV_W), BF16), pltpu.VMEM((t, KV_W), BF16)],
        compiler_params=_cp("arbitrary", "arbitrary"),
        name="attention",
    )(z_d, z_d, cos, sin, q_g, k_g, hones)


def _layer_norm(x, g, b):
    mu = jnp.mean(x, axis=-1, keepdims=True)
    xc = x - mu
    var = jnp.mean(xc * xc, axis=-1, keepdims=True)
    return xc * lax.rsqrt(var + LN_EPS) * g + b


def _group_norm(y, eps, hones):
    mu = _headsum(y, hones) * (1.0 / HEAD_DIM)
    yc = y - mu
    var = _headsum(yc * yc, hones) * (1.0 / HEAD_DIM)
    return yc * lax.rsqrt(var + eps)


def _merge_kernel(alpha, n_lat_tiles, xl_ref, xc_ref, mod_ref,
                  ry0, ry1, rbonus, rgate, ty0, ty1, tg, hy0, hy1, hg, at_ref,
                  rln_g, rln_b, tn_g, tn_b, hn_g, wg_ref, wb_ref, wo_ref, ln_g, ln_b, wr_ref, hones_ref,
                  x1_ref, u2_ref, aff_ref):
    d = xl_ref.shape[-1]
    mod = mod_ref[0, 0]
    x = jnp.where(pl.program_id(1) < n_lat_tiles, xl_ref[0], xc_ref[0])
    u = (x * (1.0 + mod[:, d:2 * d]) + mod[:, 0:d]).astype(BF16)
    hy = hy0[0] + hy1[0]
    hones = hones_ref[...]
    branches = (
        (_group_norm(ry0[0] + ry1[0], RWKV_GN_EPS, hones) * rln_g[...] + rln_b[...] + rbonus[0]) * rgate[0],
        (_group_norm(ty0[0] + ty1[0], LN_EPS, hones) * tn_g[...] + tn_b[...]) * _silu(tg[0]),
        hy * lax.rsqrt(_headsum(hy * hy, hones) * (1.0 / HEAD_DIM) + RMS_EPS) * hn_g[...] * _silu(hg[0]),
        at_ref[0],
    )
    merged = None
    for i, br in enumerate(branches):
        term = _sigmoid(jnp.dot(u, wg_ref[i], preferred_element_type=F32)) * _mm(br, wb_ref[i])
        merged = term if merged is None else merged + term
    mix = _mm(merged, wo_ref[...])
    x1 = _layer_norm(alpha * x + mod[:, 2 * d:3 * d] * mix, ln_g[...], ln_b[...])
    x1_ref[0] = x1
    u2 = (x1 * (1.0 + mod[:, 4 * d:5 * d]) + mod[:, 3 * d:4 * d]).astype(BF16)
    u2_ref[0] = u2
    logits = lax.dot_general(wr_ref[...], u2, (((1,), (1,)), ((), ())), preferred_element_type=F32)
    e = jnp.exp(logits - jnp.max(logits, axis=0, keepdims=True))
    aff_ref[0] = e / jnp.sum(e, axis=0, keepdims=True)


def _merge(alpha, x_lat, x_ctx, modsel, rw, rt, z_b, hg, z_c, att, prm, tm, n_lat_tiles, n_tiles):
    b, _, d = x_lat.shape
    rows = n_tiles * tm

    def tile(w):
        return pl.BlockSpec((1, tm, w), lambda i, j: (i, j, 0))

    def colblock(k):
        return pl.BlockSpec((1, tm, MIX_W), lambda i, j: (i, j, k))

    params = [prm["rwkv_ln_g"], prm["rwkv_ln_b"], prm["ret_norm_g"], prm["ret_norm_b"], prm["hgrn_norm_g"],
              prm["w_gate"], prm["w_branch"], prm["w_out"], prm["ln1_g"], prm["ln1_b"], prm["w_router_t"],
              prm["hones"]]
    return pl.pallas_call(
        functools.partial(_merge_kernel, alpha, n_lat_tiles),
        grid=(b, n_tiles),
        in_specs=_two_stream_specs(tm, d, n_lat_tiles)
        + [pl.BlockSpec((1, 1, 1, 6 * d), lambda i, j: (i, (j >= n_lat_tiles).astype(jnp.int32), 0, 0))]
        + [tile(MIX_W)] * 4 + [tile(MIX_W)] * 2 + [colblock(3)] + [tile(MIX_W)] * 2 + [colblock(4)] + [tile(MIX_W)]
        + [_full_spec(a) for a in params],
        out_specs=[tile(d), tile(d), pl.BlockSpec((1, N_EXPERTS, tm), lambda i, j: (i, 0, j))],
        out_shape=[jax.ShapeDtypeStruct((b, rows, d), F32), jax.ShapeDtypeStruct((b, rows, d), BF16),
                   jax.ShapeDtypeStruct((b, N_EXPERTS, rows), F32)],
        compiler_params=_cp("arbitrary", "arbitrary"),
        name="merge",
    )(x_lat, x_ctx, modsel, *rw, rt[0], rt[1], z_b, hg[0], hg[1], z_c, att, *params)


def _route_kernel(start, n, cap, aff_ref, code_ref, gate_ref):
    a = aff_ref[0][:, start:start + n]
    bits = pltpu.bitcast(a, jnp.int32)

    def bisect(_, lohi):
        lo, hi = lohi
        mid = lo + ((hi - lo + 1) >> 1)
        ok = jnp.sum(jnp.where(bits >= mid, 1.0, 0.0), axis=1, keepdims=True) >= cap
        return jnp.where(ok, mid, lo), jnp.where(ok, hi, mid - 1)

    lo0 = jnp.zeros((N_EXPERTS, 1), jnp.int32)
    thr, _ = lax.fori_loop(0, 31, bisect, (lo0, jnp.full((N_EXPERTS, 1), 0x7F800000, jnp.int32)))
    gt, eq = bits > thr, bits == thr
    need = cap - jnp.sum(jnp.where(gt, 1.0, 0.0), axis=1, keepdims=True)
    blk = min(n, ROUTE_BLOCK)
    upper = jnp.where(_iota((blk, blk), 0) <= _iota((blk, blk), 1), 1.0, 0.0).astype(BF16)

    def prefix_count(m):
        parts, running = [], jnp.zeros((N_EXPERTS, 1), F32)
        for o in range(0, n, blk):
            pre = jnp.dot(jnp.where(m[:, o:o + blk], 1.0, 0.0).astype(BF16), upper, preferred_element_type=F32)
            parts.append(pre + running)
            running = running + pre[:, blk - 1:blk]
        return jnp.concatenate(parts, axis=1)

    sel = gt | (eq & (prefix_count(eq) <= need))
    rank = prefix_count(sel) - 1.0
    code_ref[0] = jnp.where(sel, rank.astype(jnp.int32), -1)
    gate_ref[0] = a


def _route(aff, start, n, cap):
    b, e, rows = aff.shape
    code, gate = pl.pallas_call(
        functools.partial(_route_kernel, start, n, cap),
        grid=(b,),
        in_specs=[pl.BlockSpec((1, e, rows), lambda i: (i, 0, 0))],
        out_specs=[pl.BlockSpec((1, e, n), lambda i: (i, 0, 0))] * 2,
        out_shape=[jax.ShapeDtypeStruct((b, e, n), jnp.int32), jax.ShapeDtypeStruct((b, e, n), F32)],
        compiler_params=_cp("arbitrary"),
        name="route",
    )(aff)
    return code.reshape(b, e, 1, n), gate.reshape(b, e, 1, n)


def _moe_kernel(bg, cap, code_ref, gate_ref, u_ref, w1_ref, w3_ref, w2_ref, o_ref):
    @pl.when(pl.program_id(1) == 0)
    def _():
        o_ref[...] = jnp.zeros_like(o_ref)

    n = u_ref.shape[1]
    slot = _iota((cap, n), 0)
    hits = [slot == code_ref[s, 0] for s in range(bg)]
    xs = jnp.concatenate([jnp.dot(jnp.where(h, 1.0, 0.0).astype(BF16), u_ref[s], preferred_element_type=F32)
                          for s, h in enumerate(hits)], axis=0).astype(BF16)
    hmid = _silu(jnp.dot(xs, w1_ref[0], preferred_element_type=F32)) * jnp.dot(xs, w3_ref[0],
                                                                               preferred_element_type=F32)
    y = jnp.dot(hmid.astype(BF16), w2_ref[0], preferred_element_type=F32)
    for s, h in enumerate(hits):
        weights = jnp.where(h, gate_ref[s, 0], 0.0)
        o_ref[s] += _mm_tn(weights, y[s * cap:(s + 1) * cap])


def _moe(code, gate, u2, row_block, n, cap, bg, w1, w3, w2):
    b = u2.shape[0]
    d = u2.shape[-1]
    f = w1.shape[-1]
    return pl.pallas_call(
        functools.partial(_moe_kernel, bg, cap),
        grid=(b // bg, N_EXPERTS),
        in_specs=[pl.BlockSpec((bg, 1, 1, n), lambda i, e: (i, e, 0, 0)),
                  pl.BlockSpec((bg, 1, 1, n), lambda i, e: (i, e, 0, 0)),
                  pl.BlockSpec((bg, n, d), lambda i, e: (i, row_block, 0)),
                  pl.BlockSpec((1, d, f), lambda i, e: (e, 0, 0)),
                  pl.BlockSpec((1, d, f), lambda i, e: (e, 0, 0)),
                  pl.BlockSpec((1, f, d), lambda i, e: (e, 0, 0))],
        out_specs=pl.BlockSpec((bg, n, d), lambda i, e: (i, 0, 0)),
        out_shape=jax.ShapeDtypeStruct((b, n, d), F32),
        compiler_params=_cp("arbitrary", "arbitrary"),
        name="expert_ffn",
    )(code, gate, u2, w1, w3, w2)


def _cast_kernel(a_ref, b_ref, c_ref, oa_ref, ob_ref, oc_ref):
    oa_ref[0] = a_ref[0, 0].astype(BF16)
    ob_ref[0] = b_ref[0, 0].astype(BF16)
    oc_ref[0] = c_ref[0, 0].astype(BF16)


def _cast_expert_weights(w1, w3, w2, layer):
    e = w1.shape[1]
    src = lambda w: pl.BlockSpec((1, 1) + w.shape[2:], lambda i: (layer, i, 0, 0))
    dst = lambda w: pl.BlockSpec((1,) + w.shape[2:], lambda i: (i, 0, 0))
    return pl.pallas_call(
        _cast_kernel,
        grid=(e,),
        in_specs=[src(w1), src(w3), src(w2)],
        out_specs=[dst(w1), dst(w3), dst(w2)],
        out_shape=[jax.ShapeDtypeStruct(w.shape[1:], BF16) for w in (w1, w3, w2)],
        compiler_params=_cp("arbitrary"),
        name="cast_expert_weights",
    )(w1, w3, w2)


def _ln2_kernel(alpha, x_ref, f_ref, mod_ref, g_ref, b_ref, o_ref):
    d = x_ref.shape[-1]
    o_ref[0] = _layer_norm(alpha * x_ref[0] + mod_ref[0, 0][:, 5 * d:6 * d] * f_ref[0], g_ref[...], b_ref[...])


def _ln2(alpha, x1, x1_tile0, ffn, modsel, seg, g, bias, tm):
    b, rows, d = ffn.shape
    return pl.pallas_call(
        functools.partial(_ln2_kernel, alpha),
        grid=(b, rows // tm),
        in_specs=[pl.BlockSpec((1, tm, d), lambda i, j: (i, j + x1_tile0, 0)),
                  pl.BlockSpec((1, tm, d), lambda i, j: (i, j, 0)),
                  pl.BlockSpec((1, 1, 1, 6 * d), lambda i, j: (i, seg, 0, 0)),
                  pl.BlockSpec((1, d), lambda i, j: (0, 0)),
                  pl.BlockSpec((1, d), lambda i, j: (0, 0))],
        out_specs=pl.BlockSpec((1, tm, d), lambda i, j: (i, j, 0)),
        out_shape=jax.ShapeDtypeStruct((b, rows, d), F32),
        compiler_params=_cp("arbitrary", "arbitrary"),
        name="ln2",
    )(x1, ffn, modsel, g, bias)


def _rope_tables(n_lat, n_ctx):
    rows = n_lat // GRID_W
    row = jnp.repeat(jnp.arange(rows), GRID_W)
    col = jnp.tile(jnp.arange(GRID_W), rows)
    n_freq = HEAD_DIM // 4
    inv = ROPE_BASE ** (-jnp.arange(n_freq, dtype=F32) / n_freq)
    ang = jnp.concatenate([row[:, None] * inv, col[:, None] * inv], axis=-1)
    cos, sin = jnp.cos(ang), jnp.sin(ang)
    cos64 = jnp.concatenate([cos, cos], axis=-1)
    sin64 = jnp.concatenate([-sin, sin], axis=-1)
    cos64 = jnp.concatenate([cos64, jnp.ones((n_ctx, HEAD_DIM), F32)], axis=0)
    sin64 = jnp.concatenate([sin64, jnp.zeros((n_ctx, HEAD_DIM), F32)], axis=0)
    return jnp.tile(cos64, (1, MIX_HEADS)), jnp.tile(sin64, (1, MIX_HEADS))


def kernel(x, c, ctx, c_ctx, w_mod, b_mod, w_in, rwkv_mu, rwkv_w0, rwkv_w2, rwkv_a0, rwkv_a2, rwkv_g2, rwkv_kk, rwkv_ka, rwkv_rk, rwkv_ln_g, rwkv_ln_b, ret_decay, ret_norm_g, ret_norm_b, hgrn_lb, hgrn_norm_g, attn_q_g, attn_k_g, w_gate, w_branch, w_out, ln1_g, ln1_b, w_router, w_e1, w_e3, w_e2, ln2_g, ln2_b):
    bsz, n_lat, d = x.shape
    n_ctx = ctx.shape[1]
    depth = w_mod.shape[0]
    t = n_lat + n_ctx
    tm = min(256, n_ctx)
    assert n_lat % tm == 0 and n_ctx % tm == 0 and tm % CHUNK == 0 and n_lat % n_ctx == 0 and bsz % REC_BATCH == 0
    n_lat_tiles, n_tiles = n_lat // tm, t // tm
    alpha = (2 * depth) ** 0.25

    cos, sin = _rope_tables(n_lat, n_ctx)
    lb_w = jax.nn.softmax(hgrn_lb.astype(F32), axis=1)
    lower = jnp.cumsum(lb_w, axis=1) - lb_w[:, :1]
    log_lower = jnp.log(jnp.maximum(lower, HGRN_LB_FLOOR))
    log_gamma = jax.nn.log_sigmoid(ret_decay.astype(F32))
    cst = _scan_constants()

    rows = 8 * ((bsz + 1 + 7) // 8)
    cc = jnp.zeros((rows, d), F32).at[:bsz].set(c).at[bsz].set(c_ctx)
    mods = _modulation(cc, w_mod.astype(BF16), b_mod)

    x_lat, x_ctx = x, ctx
    row = lambda a: a.reshape(1, -1)
    for l in range(depth):
        need_ctx = l < depth - 1
        modsel = jnp.stack([mods[l, :bsz], jnp.broadcast_to(mods[l, bsz], (bsz, 6 * d))],
                           axis=1).reshape(bsz, 2, 1, 6 * d)
        z_a, z_b, z_c, z_d = _in_projection(x_lat, x_ctx, modsel, w_in[l].astype(BF16), tm, n_lat_tiles)
        rw = _rwkv_mixer(z_a, n_lat, {
            "mu": row(rwkv_mu[l]), "w0": rwkv_w0[l], "w2": rwkv_w2[l].astype(BF16), "a0": rwkv_a0[l],
            "a2": rwkv_a2[l].astype(BF16), "g2": rwkv_g2[l].astype(BF16), "k_k": row(rwkv_kk[l]),
            "k_a": row(rwkv_ka[l]), "r_k": row(rwkv_rk[l])}, cst)
        rt = _retention_mixer(z_b, n_lat, cos, sin, log_gamma[l], cst)
        hg = _hgrn_mixer(z_c, n_lat, jnp.log(1.0 - lower[:, l]), log_lower[:, l], cst)
        n_out_tiles = n_tiles if need_ctx else n_lat_tiles
        att = _attention_mixer(z_d, n_lat, tm, n_out_tiles, cos, sin,
                               row(jnp.tile(attn_q_g[l], MIX_HEADS)), row(jnp.tile(attn_k_g[l], KV_HEADS)),
                               cst["hones"])
        x1, u2, aff = _merge(alpha, x_lat, x_ctx, modsel, rw, rt, z_b, hg, z_c, att, {
            "rwkv_ln_g": row(rwkv_ln_g[l]), "rwkv_ln_b": row(rwkv_ln_b[l]), "ret_norm_g": row(ret_norm_g[l]),
            "ret_norm_b": row(ret_norm_b[l]), "hgrn_norm_g": row(hgrn_norm_g[l]),
            "w_gate": w_gate[l].astype(BF16), "w_branch": w_branch[l].astype(BF16), "w_out": w_out[l].astype(BF16),
            "ln1_g": row(ln1_g[l]), "ln1_b": row(ln1_b[l]), "w_router_t": w_router[l].T.astype(BF16),
            "hones": cst["hones"]},
            tm, n_lat_tiles, n_out_tiles)
        w1, w3, w2 = _cast_expert_weights(w_e1, w_e3, w_e2, l)
        cap_lat = EC_CAPACITY * n_lat // N_EXPERTS
        code, gate = _route(aff, 0, n_lat, cap_lat)
        ffn = _moe(code, gate, u2, 0, n_lat, cap_lat, 1, w1, w3, w2)
        x_lat = _ln2(alpha, x1, 0, ffn, modsel, 0, row(ln2_g[l]), row(ln2_b[l]),
                     LN_ROWS if n_lat % LN_ROWS == 0 else tm)
        if not need_ctx:
            return x_lat
        cap_ctx = EC_CAPACITY * n_ctx // N_EXPERTS
        code_c, gate_c = _route(aff, n_lat, n_ctx, cap_ctx)
        ffn_c = _moe(code_c, gate_c, u2, n_lat // n_ctx, n_ctx, cap_ctx, 8 if bsz % 8 == 0 else 1, w1, w3, w2)
        x_ctx = _ln2(alpha, x1, n_lat_tiles, ffn_c, modsel, 1, row(ln2_g[l]), row(ln2_b[l]), tm)
    return x_lat
```

```python
import functools

import jax
import jax.numpy as jnp
import numpy as np
from jax import lax
from jax.experimental import pallas as pl
from jax.experimental.pallas import tpu as pltpu

F32 = jnp.float32
BF16 = jnp.bfloat16

HEAD_DIM = 64
MIX_HEADS = 4
MIX_W = MIX_HEADS * HEAD_DIM
KV_HEADS = 2
KV_W = KV_HEADS * HEAD_DIM
LANES = 128
GRID_W = 64
ROPE_BASE = 10000.0
N_EXPERTS = 16
EC_CAPACITY = 2
LN_EPS = 1e-5
RMS_EPS = 1e-6
RWKV_GN_EPS = HEAD_DIM * 1e-5
HGRN_LB_FLOOR = 1e-20
LOG2_E = 1.4426950408889634
RWKV_LORA = 64
RWKV_LORA_G = 128
COLS_A = 3 * MIX_W + 4 * RWKV_LORA + RWKV_LORA_G
COLS_B = 4 * MIX_W
COLS_C = 5 * MIX_W
COLS_D = MIX_W + 2 * KV_W
CHUNK = 64
SUB_SHIFT = 3
SUB = 1 << SUB_SHIFT
EXP = MIX_HEADS * CHUNK
REC_BATCH = 16
ROUTE_BLOCK = 256
LN_ROWS = 1024
ROW_BATCH = 2
VMEM_LIMIT = 56 * 1024 * 1024


def _cp(*sem):
    return pltpu.CompilerParams(dimension_semantics=sem, vmem_limit_bytes=VMEM_LIMIT)


def _mm(a, b):
    return jnp.dot(a.astype(BF16), b.astype(BF16), preferred_element_type=F32)


def _mm_nt(a, b):
    return lax.dot_general(a.astype(BF16), b.astype(BF16), (((1,), (1,)), ((), ())), preferred_element_type=F32)


def _mm_tn(a, b):
    return lax.dot_general(a.astype(BF16), b.astype(BF16), (((0,), (0,)), ((), ())), preferred_element_type=F32)


def _split(x):
    hi = x.astype(BF16)
    lo = (x - hi.astype(F32)).astype(BF16)
    return hi, lo


def _mm_mask_l(mask_bf16, x):
    hi, lo = _split(x)
    return (jnp.dot(mask_bf16, hi, preferred_element_type=F32) + jnp.dot(mask_bf16, lo, preferred_element_type=F32))


def _mm_mask_r(x, mask_bf16):
    hi, lo = _split(x)
    return (jnp.dot(hi, mask_bf16, preferred_element_type=F32) + jnp.dot(lo, mask_bf16, preferred_element_type=F32))


def _iota(shape, dim):
    return lax.broadcasted_iota(jnp.int32, shape, dim)


def _headsum(x, hones):
    n = x.shape[-1]
    return _mm_mask_r(x, hones[0:n, 0:n])


def _sigmoid(x):
    return 1.0 / (1.0 + jnp.exp(-x))


def _silu(x):
    return x * _sigmoid(x)


def _softplus(x):
    return jnp.maximum(x, 0.0) + jnp.log(1.0 + jnp.exp(-jnp.abs(x)))


def _expand(x, headmask):
    return jnp.concatenate([x.astype(BF16)] * MIX_HEADS, axis=0) * headmask


def _scan_constants():
    rows = np.arange(EXP)
    head = rows // CHUNK
    same = head[:, None] == head[None, :]
    lane_head = np.arange(MIX_W) // HEAD_DIM
    lane_tok = np.arange(MIX_W) % HEAD_DIM
    tri, cum, pick = [], [], []
    for dirn in (0, 1):
        flip = (lambda a: a) if dirn == 0 else (lambda a: CHUNK - 1 - a)
        p64, pl64 = flip(np.arange(CHUNK))[:, None], flip(lane_tok)[None, :]
        tri.append(np.stack([pl64 < p64, pl64 <= p64]))
        cum.append(p64.T <= p64)
        pick.append(np.stack([((p64 - pl64) == dd) & ((p64 >> SUB_SHIFT) == (pl64 >> SUB_SHIFT)) for dd in range(SUB)]))
    f = lambda a: jnp.asarray(np.asarray(a, np.float32))
    return {
        "headmask": f(head[:, None] == lane_head[None, :]).astype(BF16),
        "same": f(same),
        "tri": f(np.stack(tri)),
        "subcols": f((lane_tok[None, :] >> SUB_SHIFT) == np.arange(CHUNK // SUB)[:, None]),
        "cum": f(np.stack(cum)).astype(BF16),
        "pick": f(np.stack(pick)),
        "hones": f(lane_head[:, None] == lane_head[None, :]).astype(BF16),
    }


def _skewed(chains):
    results = [None] * len(chains)
    done, rnd = set(), 0
    while len(done) < len(chains):
        for idx in range(min(rnd + 1, len(chains))):
            if idx not in done:
                try:
                    next(chains[idx])
                except StopIteration as stop:
                    results[idx] = stop.value
                    done.add(idx)
        rnd += 1
    return results


def _full_spec(a):
    return pl.BlockSpec(a.shape, lambda *_, _n=a.ndim: (0,) * _n)


def _rope(x, cos, sin):
    half = HEAD_DIM // 2
    blocks = [x[:, i:i + LANES] for i in range(0, x.shape[-1], LANES)]
    fwd = jnp.concatenate([pltpu.roll(b, half, 1) for b in blocks], axis=1)
    bwd = jnp.concatenate([pltpu.roll(b, LANES - half, 1) for b in blocks], axis=1)
    first = (_iota(x.shape, 1) & (HEAD_DIM - 1)) < half
    return x * cos + jnp.where(first, bwd, fwd) * sin


def _mod_kernel(c_ref, w_ref, b_ref, o_ref):
    o_ref[0] = _mm(_silu(c_ref[...]), w_ref[0]) + b_ref[0]


def _modulation(cc, w_mod, b_mod):
    depth, d, d6 = w_mod.shape
    tn = d6 // 4
    rows = cc.shape[0]
    return pl.pallas_call(
        _mod_kernel,
        grid=(depth, d6 // tn),
        in_specs=[pl.BlockSpec((rows, d), lambda l, j: (0, 0)),
                  pl.BlockSpec((1, d, tn), lambda l, j: (l, 0, j)),
                  pl.BlockSpec((1, 1, tn), lambda l, j: (l, 0, j))],
        out_specs=pl.BlockSpec((1, rows, tn), lambda l, j: (l, 0, j)),
        out_shape=jax.ShapeDtypeStruct((depth, rows, d6), F32),
        compiler_params=_cp("arbitrary", "arbitrary"),
        name="modulation",
    )(cc, w_mod, b_mod.reshape(depth, 1, d6))


def _modulated(x, mod, which):
    d = x.shape[-1]
    return x * (1.0 + mod[:, (3 * which + 1) * d:(3 * which + 2) * d]) + mod[:, 3 * which * d:(3 * which + 1) * d]


def _inproj_kernel(n_lat_tiles, xl_ref, xc_ref, mod_ref, w_ref, za_ref, zb_ref, zc_ref, zd_ref):
    is_lat = pl.program_id(1) < n_lat_tiles
    tm = xl_ref.shape[1]
    u = jnp.concatenate([_modulated(jnp.where(is_lat, xl_ref[s], xc_ref[s]), mod_ref[s, 0], 0)
                         for s in range(ROW_BATCH)], axis=0).astype(BF16)
    o = 0
    for ref, n in ((za_ref, COLS_A), (zb_ref, COLS_B), (zc_ref, COLS_C), (zd_ref, COLS_D)):
        z = jnp.dot(u, w_ref[:, o:o + n], preferred_element_type=F32)
        for s in range(ROW_BATCH):
            ref[s] = z[s * tm:(s + 1) * tm]
        o += n


def _two_stream_specs(tm, d, n_lat_tiles):
    return [pl.BlockSpec((ROW_BATCH, tm, d), lambda i, j: (i, jnp.minimum(j, n_lat_tiles - 1), 0)),
            pl.BlockSpec((ROW_BATCH, tm, d), lambda i, j: (i, jnp.maximum(j - n_lat_tiles, 0), 0))]


def _mod_spec(d, n_lat_tiles):
    return pl.BlockSpec((ROW_BATCH, 1, 1, 6 * d), lambda i, j: (i, (j >= n_lat_tiles).astype(jnp.int32), 0, 0))


def _in_projection(x_lat, x_ctx, modsel, w_in, tm, n_lat_tiles):
    b, n_lat, d = x_lat.shape
    t = n_lat + x_ctx.shape[1]
    d_in = w_in.shape[-1]
    cols = (COLS_A, COLS_B, COLS_C, COLS_D)
    return pl.pallas_call(
        functools.partial(_inproj_kernel, n_lat_tiles),
        grid=(b // ROW_BATCH, t // tm),
        in_specs=_two_stream_specs(tm, d, n_lat_tiles) + [_mod_spec(d, n_lat_tiles),
                                                          pl.BlockSpec((d, d_in), lambda i, j: (0, 0))],
        out_specs=[pl.BlockSpec((ROW_BATCH, tm, n), lambda i, j: (i, j, 0)) for n in cols],
        out_shape=[jax.ShapeDtypeStruct((b, t, n), F32) for n in cols],
        compiler_params=_cp("arbitrary", "arbitrary"),
        name="in_projection",
    )(x_lat, x_ctx, modsel, w_in)


def _chunk_maps(n_lat_chunks, n_chunks):
    n_ctx_chunks = n_chunks - n_lat_chunks

    def fwd(i):
        return jnp.where(i < n_ctx_chunks, n_lat_chunks + i, i - n_ctx_chunks)

    def bwd(i):
        return jnp.where(i < n_ctx_chunks, n_chunks - 1 - i, n_chunks - 1 - i)

    return fwd, bwd


def _rwkv_direction(dirn, zc, prev_row, next_row, s_ref, p, cst):
    hm = cst["headmask"][...]
    rowi = _iota(zc.shape, 0)
    up = jnp.where(rowi == 0, prev_row, pltpu.roll(zc, 1, 0))
    dn = jnp.where(rowi == CHUNK - 1, next_row, pltpu.roll(zc, CHUNK - 1, 0))
    zs = zc + p["mu"] * (0.5 * (up + dn) - zc)
    yield
    r, k, v = zs[:, 0:MIX_W], zs[:, MIX_W:2 * MIX_W], zs[:, 2 * MIX_W:3 * MIX_W]
    w_lo = 3 * MIX_W + RWKV_LORA * dirn
    a_lo = 3 * MIX_W + RWKV_LORA * (2 + dirn)
    zw, za = zs[:, w_lo:w_lo + RWKV_LORA], zs[:, a_lo:a_lo + RWKV_LORA]
    w = p["w0"][dirn:dirn + 1] + _mm(jnp.tanh(zw), p["w2"][dirn])
    lw = -jnp.exp(-_softplus(-w) - 0.5)
    yield
    a = _sigmoid(p["a0"][dirn:dirn + 1] + _mm(za, p["a2"][dirn]))
    kkf = k * p["k_k"]
    kk = kkf * lax.rsqrt(_headsum(kkf * kkf, cst["hones"][...]) + 1e-12)
    yield
    kd = k * (1.0 + (a - 1.0) * p["k_a"])
    bv = kk * a
    yield

    cum = _mm_mask_l(cst["cum"][dirn], lw)
    total = cum[CHUNK - 1:CHUNK] if dirn == 0 else cum[0:1]
    inv = jnp.exp(-cum)
    tail = jnp.exp(total - cum)
    yield
    lhs = jnp.concatenate([kk * jnp.exp(cum - lw), r * jnp.exp(cum)], axis=0)
    g_k = _mm_nt(lhs, _expand(kd * inv, hm))
    yield
    g_b = _mm_nt(lhs, _expand(bv * inv, hm))
    yield
    strict, incl = cst["tri"][dirn, 0], cst["tri"][dirn, 1]
    m_b = g_b[0:CHUNK] * strict
    n_b = g_b[CHUNK:2 * CHUNK] * incl
    mn_k = g_k * jnp.concatenate([strict, incl], axis=0)

    st = s_ref[...]
    carry = _mm_nt(lhs, st) + _mm(mn_k, _expand(v, hm))
    x = carry[0:CHUNK]
    yield
    x = x - _mm(m_b, _expand(x, hm))
    pw = _mm(m_b, _expand(m_b, hm))
    yield
    for step in range(5):
        x = x + _mm(pw, _expand(x, hm))
        if step < 4:
            pw = _mm(pw, _expand(pw, hm))
        yield
    y = carry[CHUNK:2 * CHUNK] - _mm(n_b, _expand(x, hm))
    yield
    upd = _mm_tn(jnp.concatenate([v, x], axis=0), jnp.concatenate([kd * tail, -(bv * tail)], axis=0))
    s_ref[...] = st * jnp.exp(total) + upd * cst["same"][...]
    return y, r, k, v, zs[:, COLS_A - RWKV_LORA_G:COLS_A]


def _rwkv_kernel(n_lat_chunks, n_chunks,
                 zf_ref, zfp_ref, zfn_ref, zb_ref, zbp_ref, zbn_ref,
                 mu_ref, w0_ref, w2_ref, a0_ref, a2_ref, g2_ref, kk_ref, ka_ref, rk_ref,
                 hm_ref, same_ref, tri_ref, cum_ref, hones_ref,
                 y0_ref, y1_ref, bonus_ref, gate_ref, s_ref):
    i = pl.program_id(1)
    fwd, bwd = _chunk_maps(n_lat_chunks, n_chunks)

    @pl.when(i == 0)
    def _():
        s_ref[...] = jnp.zeros_like(s_ref)

    p = {"mu": mu_ref[...], "w0": w0_ref[...], "w2": w2_ref, "a0": a0_ref[...], "a2": a2_ref,
         "k_k": kk_ref[...], "k_a": ka_ref[...]}
    cst = {"headmask": hm_ref, "same": same_ref, "tri": tri_ref, "cum": cum_ref, "hones": hones_ref}
    chains = []
    for s in range(REC_BATCH):
        for dirn, (z_ref, zp_ref, zn_ref) in enumerate(((zf_ref, zfp_ref, zfn_ref), (zb_ref, zbp_ref, zbn_ref))):
            c = fwd(i) if dirn == 0 else bwd(i)
            first = jnp.logical_or(c == 0, c == n_lat_chunks)
            last = jnp.logical_or(c == n_lat_chunks - 1, c == n_chunks - 1)
            prev_row = jnp.where(first, 0.0, zp_ref[s][7:8, :])
            next_row = jnp.where(last, 0.0, zn_ref[s][0:1, :])
            chains.append(_rwkv_direction(dirn, z_ref[s], prev_row, next_row, s_ref.at[s, dirn], p, cst))
    for idx, (y, r, k, v, zg) in enumerate(_skewed(chains)):
        s, dirn = divmod(idx, 2)
        if dirn == 0:
            y0_ref[s] = y
            bonus_ref[s] = _headsum(r * k * rk_ref[...], hones_ref[...]) * v
            gate_ref[s] = _mm(_sigmoid(zg), g2_ref[...])
        else:
            y1_ref[s] = y


def _rwkv_mixer(z_a, n_lat, prm, cst):
    b, t, _ = z_a.shape
    n_chunks, n_lat_chunks = t // CHUNK, n_lat // CHUNK
    fwd, bwd = _chunk_maps(n_lat_chunks, n_chunks)
    per8 = CHUNK // 8
    last8 = t // 8 - 1
    rb = REC_BATCH

    def zspecs(cm):
        return [pl.BlockSpec((rb, CHUNK, COLS_A), lambda bi, i: (bi, cm(i), 0)),
                pl.BlockSpec((rb, 8, COLS_A), lambda bi, i: (bi, jnp.maximum(cm(i) * per8 - 1, 0), 0)),
                pl.BlockSpec((rb, 8, COLS_A), lambda bi, i: (bi, jnp.minimum((cm(i) + 1) * per8, last8), 0))]

    params = [prm["mu"], prm["w0"], prm["w2"], prm["a0"], prm["a2"], prm["g2"], prm["k_k"], prm["k_a"], prm["r_k"],
              cst["headmask"], cst["same"], cst["tri"], cst["cum"], cst["hones"]]
    yspec_f = pl.BlockSpec((rb, CHUNK, MIX_W), lambda bi, i: (bi, fwd(i), 0))
    yspec_b = pl.BlockSpec((rb, CHUNK, MIX_W), lambda bi, i: (bi, bwd(i), 0))
    shape = jax.ShapeDtypeStruct((b, t, MIX_W), F32)
    return pl.pallas_call(
        functools.partial(_rwkv_kernel, n_lat_chunks, n_chunks),
        grid=(b // rb, n_chunks),
        in_specs=zspecs(fwd) + zspecs(bwd) + [_full_spec(a) for a in params],
        out_specs=[yspec_f, yspec_b, yspec_f, yspec_f],
        out_shape=[shape] * 4,
        scratch_shapes=[pltpu.VMEM((rb, 2, MIX_W, MIX_W), F32)],
        compiler_params=_cp("arbitrary", "arbitrary"),
        name="rwkv7",
    )(z_a, z_a, z_a, z_a, z_a, z_a, *params)


def _ret_direction(dirn, zc, cos, sin, lg, intra, chunk_decay, r_ref, hm, same):
    q = _rope(zc[:, 0:MIX_W], cos, sin)
    k = _rope(zc[:, MIX_W:2 * MIX_W], cos, sin) * HEAD_DIM ** -0.5
    v = zc[:, 2 * MIX_W:3 * MIX_W]
    pos = _iota((CHUNK, MIX_W), 0).astype(F32)
    if dirn == 1:
        pos = (CHUNK - 1.0) - pos
    yield
    sc = _mm_nt(q, _expand(k, hm)) * intra
    yield
    rs = r_ref[...]
    o = _mm(sc, _expand(v, hm)) + _mm(q * jnp.exp(lg * (pos + 1.0)), rs)
    yield
    r_ref[...] = rs * chunk_decay + _mm_tn(k * jnp.exp(lg * ((CHUNK - 1.0) - pos)), v) * same
    return o


def _ret_kernel(zf_ref, zb_ref, cf_ref, sf_ref, cb_ref, sb_ref, lg_ref, intra_ref, cd_ref, hm_ref, same_ref,
                y0_ref, y1_ref, r_ref):
    @pl.when(pl.program_id(1) == 0)
    def _():
        r_ref[...] = jnp.zeros_like(r_ref)

    hm, same = hm_ref[...], same_ref[...]
    chains = []
    for s in range(REC_BATCH):
        chains.append(_ret_direction(0, zf_ref[s], cf_ref[...], sf_ref[...], lg_ref[0:1], intra_ref[0], cd_ref[0],
                                     r_ref.at[s, 0], hm, same))
        chains.append(_ret_direction(1, zb_ref[s], cb_ref[...], sb_ref[...], lg_ref[1:2], intra_ref[1], cd_ref[1],
                                     r_ref.at[s, 1], hm, same))
    for idx, o in enumerate(_skewed(chains)):
        s, dirn = divmod(idx, 2)
        (y0_ref if dirn == 0 else y1_ref)[s] = o


def _retention_tables(log_gamma, cst):
    lane_tok = np.arange(MIX_W) % HEAD_DIM
    dist = np.abs(np.arange(CHUNK)[:, None] - lane_tok[None, :]).astype(np.float32)
    lg_lanes = jnp.repeat(log_gamma, HEAD_DIM, axis=-1)
    intra = jnp.exp(lg_lanes[:, None, :] * dist) * cst["tri"][:, 1]
    return intra, jnp.exp(lg_lanes[:, :, None] * float(CHUNK)) * cst["same"]


def _retention_mixer(z_b, n_lat, cos, sin, log_gamma, cst):
    b, t, _ = z_b.shape
    n_chunks, n_lat_chunks = t // CHUNK, n_lat // CHUNK
    fwd, bwd = _chunk_maps(n_lat_chunks, n_chunks)
    shape = jax.ShapeDtypeStruct((b, t, MIX_W), F32)
    rb = REC_BATCH
    intra, chunk_decay = _retention_tables(log_gamma, cst)
    lg_lanes = jnp.repeat(log_gamma, HEAD_DIM, axis=-1)

    def zs(cm):
        return pl.BlockSpec((rb, CHUNK, COLS_B), lambda bi, i: (bi, cm(i), 0))

    def ts(cm):
        return pl.BlockSpec((CHUNK, MIX_W), lambda bi, i: (cm(i), 0))

    consts = [lg_lanes, intra, chunk_decay, cst["headmask"], cst["same"]]
    return pl.pallas_call(
        _ret_kernel,
        grid=(b // rb, n_chunks),
        in_specs=[zs(fwd), zs(bwd), ts(fwd), ts(fwd), ts(bwd), ts(bwd)] + [_full_spec(a) for a in consts],
        out_specs=[pl.BlockSpec((rb, CHUNK, MIX_W), lambda bi, i: (bi, fwd(i), 0)),
                   pl.BlockSpec((rb, CHUNK, MIX_W), lambda bi, i: (bi, bwd(i), 0))],
        out_shape=[shape] * 2,
        scratch_shapes=[pltpu.VMEM((rb, 2, MIX_W, MIX_W), F32)],
        compiler_params=_cp("arbitrary", "arbitrary"),
        name="retention",
    )(z_b, z_b, cos, sin, cos, sin, *consts)


def _hgrn_direction(dirn, zc, log_keep, log_lb, s_ref, cst):
    hm = cst["headmask"][...]
    qs = _silu(zc[:, 0:MIX_W])
    fz = zc[:, MIX_W * (1 + dirn):MIX_W * (2 + dirn)]
    v = zc[:, 3 * MIX_W:4 * MIX_W]
    tail_term = jnp.log(1.0 + jnp.exp(-jnp.abs(fz)))
    ls_pos = -(jnp.maximum(-fz, 0.0) + tail_term)
    ls_neg = -(jnp.maximum(fz, 0.0) + tail_term)
    p1, p2 = ls_pos, log_lb + ls_neg
    log_f = jnp.maximum(p1, p2) + jnp.log(1.0 + jnp.exp(-jnp.abs(p1 - p2)))
    yield
    bc = _mm_mask_l(cst["cum"][dirn], log_f * LOG2_E)
    total = bc[CHUNK - 1:CHUNK] if dirn == 0 else bc[0:1]
    excl = bc - log_f * LOG2_E
    lk = (log_keep + ls_neg) * LOG2_E - bc

    nsub = CHUNK // SUB
    first_tok = [b * SUB if dirn == 0 else b * SUB + SUB - 1 for b in range(nsub)]
    last_tok = [b * SUB + SUB - 1 if dirn == 0 else b * SUB for b in range(nsub)]
    e_start = [excl[i:i + 1] for i in first_tok]
    e_end = [bc[i:i + 1] for i in last_tok]
    row_ref = jnp.concatenate([jnp.broadcast_to(e, (SUB, MIX_W)) for e in e_start], axis=0)
    key_ref = jnp.concatenate([jnp.broadcast_to(e, (SUB, MIX_W)) for e in e_end], axis=0)
    q_rel = qs * jnp.exp2(bc - row_ref)
    k_rel = jnp.exp2(key_ref + lk)
    yield
    pairs = [(bi, bj) for bi in range(nsub) for bj in range(nsub) if (bj < bi if dirn == 0 else bj > bi)]
    lhs = jnp.concatenate([q_rel[bi * SUB:(bi + 1) * SUB] * jnp.exp2(e_start[bi] - e_end[bj]) for bi, bj in pairs],
                          axis=0)
    g = _mm_nt(lhs, _expand(k_rel, hm))
    yield
    acc = [None] * nsub
    for idx, (bi, bj) in enumerate(pairs):
        part = g[idx * SUB:(idx + 1) * SUB] * cst["subcols"][bj:bj + 1]
        acc[bi] = part if acc[bi] is None else acc[bi] + part
    att = jnp.concatenate([a if a is not None else jnp.zeros((SUB, MIX_W), F32) for a in acc], axis=0)
    yield
    prods = []
    for dd in range(SUB):
        sh = dd if dirn == 0 else (CHUNK - dd) % CHUNK
        lr = lk if dd == 0 else pltpu.roll(lk, sh, 0)
        prods.append((qs * jnp.exp2(jnp.minimum(bc + lr, 0.0))).astype(BF16))
        if dd % 2 == 1:
            yield
    val = jnp.dot(jnp.concatenate(prods, axis=0), cst["hones"][...], preferred_element_type=F32)
    yield
    for dd in range(SUB):
        att = att + val[dd * CHUNK:(dd + 1) * CHUNK] * cst["pick"][dirn, dd]

    st = s_ref[...]
    yield
    o = _mm(att, _expand(v, hm)) + _mm_nt(qs * jnp.exp2(bc), st)
    yield
    s_ref[...] = st * jnp.exp2(total) + _mm_tn(v, jnp.exp2(total + lk)) * cst["same"][...]
    return o


def _hgrn_kernel(zf_ref, zb_ref, keep_ref, llb_ref, hm_ref, same_ref, cum_ref, subcols_ref, pick_ref, hones_ref,
                 y0_ref, y1_ref, s_ref):
    @pl.when(pl.program_id(1) == 0)
    def _():
        s_ref[...] = jnp.zeros_like(s_ref)

    cst = {"headmask": hm_ref, "same": same_ref, "cum": cum_ref, "subcols": subcols_ref, "pick": pick_ref,
           "hones": hones_ref}
    chains = []
    for s in range(REC_BATCH):
        chains.append(_hgrn_direction(0, zf_ref[s], keep_ref[0:1], llb_ref[0:1], s_ref.at[s, 0], cst))
        chains.append(_hgrn_direction(1, zb_ref[s], keep_ref[1:2], llb_ref[1:2], s_ref.at[s, 1], cst))
    for idx, o in enumerate(_skewed(chains)):
        s, dirn = divmod(idx, 2)
        (y0_ref if dirn == 0 else y1_ref)[s] = o


def _hgrn_mixer(z_c, n_lat, log_keep, log_lb, cst):
    b, t, _ = z_c.shape
    n_chunks, n_lat_chunks = t // CHUNK, n_lat // CHUNK
    fwd, bwd = _chunk_maps(n_lat_chunks, n_chunks)
    shape = jax.ShapeDtypeStruct((b, t, MIX_W), F32)
    rb = REC_BATCH
    consts = [log_keep, log_lb, cst["headmask"], cst["same"], cst["cum"], cst["subcols"], cst["pick"], cst["hones"]]
    return pl.pallas_call(
        _hgrn_kernel,
        grid=(b // rb, n_chunks),
        in_specs=[pl.BlockSpec((rb, CHUNK, COLS_C), lambda bi, i: (bi, fwd(i), 0)),
                  pl.BlockSpec((rb, CHUNK, COLS_C), lambda bi, i: (bi, bwd(i), 0))] + [_full_spec(a) for a in consts],
        out_specs=[pl.BlockSpec((rb, CHUNK, MIX_W), lambda bi, i: (bi, fwd(i), 0)),
                   pl.BlockSpec((rb, CHUNK, MIX_W), lambda bi, i: (bi, bwd(i), 0))],
        out_shape=[shape] * 2,
        scratch_shapes=[pltpu.VMEM((rb, 2, MIX_W, MIX_W), F32)],
        compiler_params=_cp("arbitrary", "arbitrary"),
        name="hgrn2",
    )(z_c, z_c, *consts)


def _attn_kernel(tq, n_lat, zq_ref, zkv_ref, cos_ref, sin_ref, qg_ref, kg_ref, hones_ref, o_ref, k_s, v_s):
    j = pl.program_id(1)
    t = zkv_ref.shape[1]

    @pl.when(j == 0)
    def _():
        for s in range(ROW_BATCH):
            kf = zkv_ref[s][:, MIX_W:MIX_W + KV_W]
            kn = kf * lax.rsqrt(_headsum(kf * kf, hones_ref[...]) * (1.0 / HEAD_DIM) + RMS_EPS) * kg_ref[...]
            k_s[s] = _rope(kn, cos_ref[:, 0:KV_W], sin_ref[:, 0:KV_W]).astype(BF16)
            v_s[s] = zkv_ref[s][:, MIX_W + KV_W:COLS_D].astype(BF16)

    row0 = pl.multiple_of(j * tq, tq)
    qs = []
    for s in range(ROW_BATCH):
        qf = zq_ref[s][:, 0:MIX_W]
        qn = qf * lax.rsqrt(_headsum(qf * qf, hones_ref[...]) * (1.0 / HEAD_DIM) + RMS_EPS) * qg_ref[...]
        qs.append((_rope(qn, cos_ref[pl.ds(row0, tq), :], sin_ref[pl.ds(row0, tq), :])
                   * (HEAD_DIM ** -0.5 * LOG2_E)).astype(BF16))

    def head(s, h, key_lo, key_n):
        g = h // (MIX_HEADS // KV_HEADS)
        hd = HEAD_DIM
        sc = lax.dot_general(qs[s][:, h * hd:(h + 1) * hd], k_s[s, pl.ds(key_lo, key_n), g * hd:(g + 1) * hd],
                             (((1,), (1,)), ((), ())), preferred_element_type=F32)
        yield
        e = jnp.exp2(sc - jnp.max(sc, axis=1, keepdims=True))
        den = jnp.sum(e, axis=1, keepdims=True)
        e = e.astype(BF16)
        yield
        pv = jnp.dot(e, v_s[s, pl.ds(key_lo, key_n), g * hd:(g + 1) * hd], preferred_element_type=F32)
        return pv / den

    def attend(key_lo, key_n):
        outs = _skewed([head(s, h, key_lo, key_n) for s in range(ROW_BATCH) for h in range(MIX_HEADS)])
        for s in range(ROW_BATCH):
            o_ref[s] = jnp.concatenate(outs[s * MIX_HEADS:(s + 1) * MIX_HEADS], axis=1)

    @pl.when(row0 < n_lat)
    def _():
        attend(0, t)

    @pl.when(row0 >= n_lat)
    def _():
        attend(n_lat, t - n_lat)


def _attention_mixer(z_d, n_lat, tq, nq, cos, sin, q_g, k_g, hones):
    b, t, _ = z_d.shape
    return pl.pallas_call(
        functools.partial(_attn_kernel, tq, n_lat),
        grid=(b // ROW_BATCH, nq),
        in_specs=[pl.BlockSpec((ROW_BATCH, tq, COLS_D), lambda bi, j: (bi, j, 0)),
                  pl.BlockSpec((ROW_BATCH, t, COLS_D), lambda bi, j: (bi, 0, 0)),
                  pl.BlockSpec((t, MIX_W), lambda bi, j: (0, 0)),
                  pl.BlockSpec((t, MIX_W), lambda bi, j: (0, 0)),
                  pl.BlockSpec((1, MIX_W), lambda bi, j: (0, 0)),
                  pl.BlockSpec((1, KV_W), lambda bi, j: (0, 0)),
                  _full_spec(hones)],
        out_specs=pl.BlockSpec((ROW_BATCH, tq, MIX_W), lambda bi, j: (bi, j, 0)),
        out_shape=jax.ShapeDtypeStruct((b, nq * tq, MIX_W), F32),
        scratch_shapes=[pltpu.VMEM((ROW_BATCH, t, KV_W), BF16), pltpu.VMEM((ROW_BATCH, t, KV_W), BF16)],
        compiler_params=_cp("arbitrary", "arbitrary"),
        name="attention",
    )(z_d, z_d, cos, sin, q_g, k_g, hones)


def _layer_norm(x, g, b):
    mu = jnp.mean(x, axis=-1, keepdims=True)
    xc = x - mu
    var = jnp.mean(xc * xc, axis=-1, keepdims=True)
    return xc * lax.rsqrt(var + LN_EPS) * g + b


def _group_norm(y, eps, hones):
    mu = _headsum(y, hones) * (1.0 / HEAD_DIM)
    yc = y - mu
    var = _headsum(yc * yc, hones) * (1.0 / HEAD_DIM)
    return yc * lax.rsqrt(var + eps)


def _merge_kernel(alpha, n_lat_tiles, xl_ref, xc_ref, mod_ref,
                  ry0, ry1, rbonus, rgate, ty0, ty1, tg, hy0, hy1, hg, at_ref,
                  rln_g, rln_b, tn_g, tn_b, hn_g, wg_ref, wb_ref, wo_ref, ln_g, ln_b, wr_ref, hones_ref,
                  x1_ref, u2_ref, aff_ref):
    d = xl_ref.shape[-1]
    tm = xl_ref.shape[1]
    is_lat = pl.program_id(1) < n_lat_tiles
    stack = lambda ref: jnp.concatenate([ref[s] for s in range(ROW_BATCH)], axis=0)
    per_row = lambda lo: jnp.concatenate([jnp.broadcast_to(mod_ref[s, 0][:, lo * d:(lo + 1) * d], (tm, d))
                                          for s in range(ROW_BATCH)], axis=0)
    x = jnp.where(is_lat, stack(xl_ref), stack(xc_ref))
    u = (x * (1.0 + per_row(1)) + per_row(0)).astype(BF16)
    hy = stack(hy0) + stack(hy1)
    hones = hones_ref[...]
    branches = (
        (_group_norm(stack(ry0) + stack(ry1), RWKV_GN_EPS, hones) * rln_g[...] + rln_b[...] + stack(rbonus))
        * stack(rgate),
        (_group_norm(stack(ty0) + stack(ty1), LN_EPS, hones) * tn_g[...] + tn_b[...]) * _silu(stack(tg)),
        hy * lax.rsqrt(_headsum(hy * hy, hones) * (1.0 / HEAD_DIM) + RMS_EPS) * hn_g[...] * _silu(stack(hg)),
        stack(at_ref),
    )
    merged = None
    for i, br in enumerate(branches):
        term = _sigmoid(jnp.dot(u, wg_ref[i], preferred_element_type=F32)) * _mm(br, wb_ref[i])
        merged = term if merged is None else merged + term
    mix = _mm(merged, wo_ref[...])
    x1 = _layer_norm(alpha * x + per_row(2) * mix, ln_g[...], ln_b[...])
    u2 = (x1 * (1.0 + per_row(4)) + per_row(3)).astype(BF16)
    logits = lax.dot_general(wr_ref[...], u2, (((1,), (1,)), ((), ())), preferred_element_type=F32)
    e = jnp.exp(logits - jnp.max(logits, axis=0, keepdims=True))
    aff = e / jnp.sum(e, axis=0, keepdims=True)
    for s in range(ROW_BATCH):
        x1_ref[s] = x1[s * tm:(s + 1) * tm]
        u2_ref[s] = u2[s * tm:(s + 1) * tm]
        aff_ref[s] = aff[:, s * tm:(s + 1) * tm]


def _merge(alpha, x_lat, x_ctx, modsel, rw, rt, z_b, hg, z_c, att, prm, tm, n_lat_tiles, n_tiles):
    b, _, d = x_lat.shape
    rows = n_tiles * tm

    def tile(w):
        return pl.BlockSpec((ROW_BATCH, tm, w), lambda i, j: (i, j, 0))

    def colblock(k):
        return pl.BlockSpec((ROW_BATCH, tm, MIX_W), lambda i, j: (i, j, k))

    params = [prm["rwkv_ln_g"], prm["rwkv_ln_b"], prm["ret_norm_g"], prm["ret_norm_b"], prm["hgrn_norm_g"],
              prm["w_gate"], prm["w_branch"], prm["w_out"], prm["ln1_g"], prm["ln1_b"], prm["w_router_t"],
              prm["hones"]]
    return pl.pallas_call(
        functools.partial(_merge_kernel, alpha, n_lat_tiles),
        grid=(b // ROW_BATCH, n_tiles),
        in_specs=_two_stream_specs(tm, d, n_lat_tiles) + [_mod_spec(d, n_lat_tiles)]
        + [tile(MIX_W)] * 4 + [tile(MIX_W)] * 2 + [colblock(3)] + [tile(MIX_W)] * 2 + [colblock(4)] + [tile(MIX_W)]
        + [_full_spec(a) for a in params],
        out_specs=[tile(d), tile(d), pl.BlockSpec((ROW_BATCH, N_EXPERTS, tm), lambda i, j: (i, 0, j))],
        out_shape=[jax.ShapeDtypeStruct((b, rows, d), F32), jax.ShapeDtypeStruct((b, rows, d), BF16),
                   jax.ShapeDtypeStruct((b, N_EXPERTS, rows), F32)],
        compiler_params=_cp("arbitrary", "arbitrary"),
        name="merge",
    )(x_lat, x_ctx, modsel, *rw, rt[0], rt[1], z_b, hg[0], hg[1], z_c, att, *params)


def _route_kernel(start, n, cap, aff_ref, code_ref, gate_ref):
    a = aff_ref[0][:, start:start + n]
    bits = pltpu.bitcast(a, jnp.int32)

    def bisect(_, lohi):
        lo, hi = lohi
        mid = lo + ((hi - lo + 1) >> 1)
        ok = jnp.sum(jnp.where(bits >= mid, 1.0, 0.0), axis=1, keepdims=True) >= cap
        return jnp.where(ok, mid, lo), jnp.where(ok, hi, mid - 1)

    lo0 = jnp.zeros((N_EXPERTS, 1), jnp.int32)
    thr, _ = lax.fori_loop(0, 31, bisect, (lo0, jnp.full((N_EXPERTS, 1), 0x7F800000, jnp.int32)))
    gt, eq = bits > thr, bits == thr
    need = cap - jnp.sum(jnp.where(gt, 1.0, 0.0), axis=1, keepdims=True)
    blk = min(n, ROUTE_BLOCK)
    upper = jnp.where(_iota((blk, blk), 0) <= _iota((blk, blk), 1), 1.0, 0.0).astype(BF16)

    def prefix_count(m):
        parts, running = [], jnp.zeros((N_EXPERTS, 1), F32)
        for o in range(0, n, blk):
            pre = jnp.dot(jnp.where(m[:, o:o + blk], 1.0, 0.0).astype(BF16), upper, preferred_element_type=F32)
            parts.append(pre + running)
            running = running + pre[:, blk - 1:blk]
        return jnp.concatenate(parts, axis=1)

    sel = gt | (eq & (prefix_count(eq) <= need))
    rank = prefix_count(sel) - 1.0
    code_ref[0] = jnp.where(sel, rank.astype(jnp.int32), -1)
    gate_ref[0] = a


def _route(aff, start, n, cap):
    b, e, rows = aff.shape
    code, gate = pl.pallas_call(
        functools.partial(_route_kernel, start, n, cap),
        grid=(b,),
        in_specs=[pl.BlockSpec((1, e, rows), lambda i: (i, 0, 0))],
        out_specs=[pl.BlockSpec((1, e, n), lambda i: (i, 0, 0))] * 2,
        out_shape=[jax.ShapeDtypeStruct((b, e, n), jnp.int32), jax.ShapeDtypeStruct((b, e, n), F32)],
        compiler_params=_cp("arbitrary"),
        name="route",
    )(aff)
    return code.reshape(b, e, 1, n), gate.reshape(b, e, 1, n)


def _moe_kernel(bg, cap, code_ref, gate_ref, u_ref, w1_ref, w3_ref, w2_ref, o_ref):
    @pl.when(pl.program_id(1) == 0)
    def _():
        o_ref[...] = jnp.zeros_like(o_ref)

    n = u_ref.shape[1]
    slot = _iota((cap, n), 0)
    hits = [slot == code_ref[s, 0] for s in range(bg)]
    xs = jnp.concatenate([jnp.dot(jnp.where(h, 1.0, 0.0).astype(BF16), u_ref[s], preferred_element_type=F32)
                          for s, h in enumerate(hits)], axis=0).astype(BF16)
    hmid = _silu(jnp.dot(xs, w1_ref[0], preferred_element_type=F32)) * jnp.dot(xs, w3_ref[0],
                                                                               preferred_element_type=F32)
    y = jnp.dot(hmid.astype(BF16), w2_ref[0], preferred_element_type=F32)
    for s, h in enumerate(hits):
        weights = jnp.where(h, gate_ref[s, 0], 0.0)
        o_ref[s] += _mm_tn(weights, y[s * cap:(s + 1) * cap])


def _moe(code, gate, u2, row_block, n, cap, bg, w1, w3, w2):
    b = u2.shape[0]
    d = u2.shape[-1]
    f = w1.shape[-1]
    return pl.pallas_call(
        functools.partial(_moe_kernel, bg, cap),
        grid=(b // bg, N_EXPERTS),
        in_specs=[pl.BlockSpec((bg, 1, 1, n), lambda i, e: (i, e, 0, 0)),
                  pl.BlockSpec((bg, 1, 1, n), lambda i, e: (i, e, 0, 0)),
                  pl.BlockSpec((bg, n, d), lambda i, e: (i, row_block, 0)),
                  pl.BlockSpec((1, d, f), lambda i, e: (e, 0, 0)),
                  pl.BlockSpec((1, d, f), lambda i, e: (e, 0, 0)),
                  pl.BlockSpec((1, f, d), lambda i, e: (e, 0, 0))],
        out_specs=pl.BlockSpec((bg, n, d), lambda i, e: (i, 0, 0)),
        out_shape=jax.ShapeDtypeStruct((b, n, d), F32),
        compiler_params=_cp("arbitrary", "arbitrary"),
        name="expert_ffn",
    )(code, gate, u2, w1, w3, w2)


def _cast_kernel(a_ref, b_ref, c_ref, oa_ref, ob_ref, oc_ref):
    oa_ref[0] = a_ref[0, 0].astype(BF16)
    ob_ref[0] = b_ref[0, 0].astype(BF16)
    oc_ref[0] = c_ref[0, 0].astype(BF16)


def _cast_expert_weights(w1, w3, w2, layer):
    e = w1.shape[1]
    src = lambda w: pl.BlockSpec((1, 1) + w.shape[2:], lambda i: (layer, i, 0, 0))
    dst = lambda w: pl.BlockSpec((1,) + w.shape[2:], lambda i: (i, 0, 0))
    return pl.pallas_call(
        _cast_kernel,
        grid=(e,),
        in_specs=[src(w1), src(w3), src(w2)],
        out_specs=[dst(w1), dst(w3), dst(w2)],
        out_shape=[jax.ShapeDtypeStruct(w.shape[1:], BF16) for w in (w1, w3, w2)],
        compiler_params=_cp("arbitrary"),
        name="cast_expert_weights",
    )(w1, w3, w2)


def _ln2_kernel(alpha, x_ref, f_ref, mod_ref, g_ref, b_ref, o_ref):
    d = x_ref.shape[-1]
    o_ref[0] = _layer_norm(alpha * x_ref[0] + mod_ref[0, 0][:, 5 * d:6 * d] * f_ref[0], g_ref[...], b_ref[...])


def _ln2(alpha, x1, x1_tile0, ffn, modsel, seg, g, bias, tm):
    b, rows, d = ffn.shape
    return pl.pallas_call(
        functools.partial(_ln2_kernel, alpha),
        grid=(b, rows // tm),
        in_specs=[pl.BlockSpec((1, tm, d), lambda i, j: (i, j + x1_tile0, 0)),
                  pl.BlockSpec((1, tm, d), lambda i, j: (i, j, 0)),
                  pl.BlockSpec((1, 1, 1, 6 * d), lambda i, j: (i, seg, 0, 0)),
                  pl.BlockSpec((1, d), lambda i, j: (0, 0)),
                  pl.BlockSpec((1, d), lambda i, j: (0, 0))],
        out_specs=pl.BlockSpec((1, tm, d), lambda i, j: (i, j, 0)),
        out_shape=jax.ShapeDtypeStruct((b, rows, d), F32),
        compiler_params=_cp("arbitrary", "arbitrary"),
        name="ln2",
    )(x1, ffn, modsel, g, bias)


def _rope_tables(n_lat, n_ctx):
    rows = n_lat // GRID_W
    row = jnp.repeat(jnp.arange(rows), GRID_W)
    col = jnp.tile(jnp.arange(GRID_W), rows)
    n_freq = HEAD_DIM // 4
    inv = ROPE_BASE ** (-jnp.arange(n_freq, dtype=F32) / n_freq)
    ang = jnp.concatenate([row[:, None] * inv, col[:, None] * inv], axis=-1)
    cos, sin = jnp.cos(ang), jnp.sin(ang)
    cos64 = jnp.concatenate([cos, cos], axis=-1)
    sin64 = jnp.concatenate([-sin, sin], axis=-1)
    cos64 = jnp.concatenate([cos64, jnp.ones((n_ctx, HEAD_DIM), F32)], axis=0)
    sin64 = jnp.concatenate([sin64, jnp.zeros((n_ctx, HEAD_DIM), F32)], axis=0)
    return jnp.tile(cos64, (1, MIX_HEADS)), jnp.tile(sin64, (1, MIX_HEADS))


def kernel(x, c, ctx, c_ctx, w_mod, b_mod, w_in, rwkv_mu, rwkv_w0, rwkv_w2, rwkv_a0, rwkv_a2, rwkv_g2, rwkv_kk, rwkv_ka, rwkv_rk, rwkv_ln_g, rwkv_ln_b, ret_decay, ret_norm_g, ret_norm_b, hgrn_lb, hgrn_norm_g, attn_q_g, attn_k_g, w_gate, w_branch, w_out, ln1_g, ln1_b, w_router, w_e1, w_e3, w_e2, ln2_g, ln2_b):
    bsz, n_lat, d = x.shape
    n_ctx = ctx.shape[1]
    depth = w_mod.shape[0]
    t = n_lat + n_ctx
    tm = min(256, n_ctx)
    assert n_lat % tm == 0 and n_ctx % tm == 0 and tm % CHUNK == 0 and n_lat % n_ctx == 0 and bsz % REC_BATCH == 0
    assert bsz % ROW_BATCH == 0
    n_lat_tiles, n_tiles = n_lat // tm, t // tm
    alpha = (2 * depth) ** 0.25

    cos, sin = _rope_tables(n_lat, n_ctx)
    lb_w = jax.nn.softmax(hgrn_lb.astype(F32), axis=1)
    lower = jnp.cumsum(lb_w, axis=1) - lb_w[:, :1]
    log_lower = jnp.log(jnp.maximum(lower, HGRN_LB_FLOOR))
    log_gamma = jax.nn.log_sigmoid(ret_decay.astype(F32))
    cst = _scan_constants()

    rows = 8 * ((bsz + 1 + 7) // 8)
    cc = jnp.zeros((rows, d), F32).at[:bsz].set(c).at[bsz].set(c_ctx)
    mods = _modulation(cc, w_mod.astype(BF16), b_mod)

    x_lat, x_ctx = x, ctx
    row = lambda a: a.reshape(1, -1)
    for l in range(depth):
        need_ctx = l < depth - 1
        modsel = jnp.stack([mods[l, :bsz], jnp.broadcast_to(mods[l, bsz], (bsz, 6 * d))],
                           axis=1).reshape(bsz, 2, 1, 6 * d)
        z_a, z_b, z_c, z_d = _in_projection(x_lat, x_ctx, modsel, w_in[l].astype(BF16), tm, n_lat_tiles)
        rw = _rwkv_mixer(z_a, n_lat, {
            "mu": row(rwkv_mu[l]), "w0": rwkv_w0[l], "w2": rwkv_w2[l].astype(BF16), "a0": rwkv_a0[l],
            "a2": rwkv_a2[l].astype(BF16), "g2": rwkv_g2[l].astype(BF16), "k_k": row(rwkv_kk[l]),
            "k_a": row(rwkv_ka[l]), "r_k": row(rwkv_rk[l])}, cst)
        rt = _retention_mixer(z_b, n_lat, cos, sin, log_gamma[l], cst)
        hg = _hgrn_mixer(z_c, n_lat, jnp.log(1.0 - lower[:, l]), log_lower[:, l], cst)
        n_out_tiles = n_tiles if need_ctx else n_lat_tiles
        att = _attention_mixer(z_d, n_lat, tm, n_out_tiles, cos, sin,
                               row(jnp.tile(attn_q_g[l], MIX_HEADS)), row(jnp.tile(attn_k_g[l], KV_HEADS)),
                               cst["hones"])
        x1, u2, aff = _merge(alpha, x_lat, x_ctx, modsel, rw, rt, z_b, hg, z_c, att, {
            "rwkv_ln_g": row(rwkv_ln_g[l]), "rwkv_ln_b": row(rwkv_ln_b[l]), "ret_norm_g": row(ret_norm_g[l]),
            "ret_norm_b": row(ret_norm_b[l]), "hgrn_norm_g": row(hgrn_norm_g[l]),
            "w_gate": w_gate[l].astype(BF16), "w_branch": w_branch[l].astype(BF16), "w_out": w_out[l].astype(BF16),
            "ln1_g": row(ln1_g[l]), "ln1_b": row(ln1_b[l]), "w_router_t": w_router[l].T.astype(BF16),
            "hones": cst["hones"]},
            tm, n_lat_tiles, n_out_tiles)
        w1, w3, w2 = _cast_expert_weights(w_e1, w_e3, w_e2, l)
        cap_lat = EC_CAPACITY * n_lat // N_EXPERTS
        code, gate = _route(aff, 0, n_lat, cap_lat)
        ffn = _moe(code, gate, u2, 0, n_lat, cap_lat, 1, w1, w3, w2)
        x_lat = _ln2(alpha, x1, 0, ffn, modsel, 0, row(ln2_g[l]), row(ln2_b[l]),
                     LN_ROWS if n_lat % LN_ROWS == 0 else tm)
        if not need_ctx:
            return x_lat
        cap_ctx = EC_CAPACITY * n_ctx // N_EXPERTS
        code_c, gate_c = _route(aff, n_lat, n_ctx, cap_ctx)
        ffn_c = _moe(code_c, gate_c, u2, n_lat // n_ctx, n_ctx, cap_ctx, 8 if bsz % 8 == 0 else 1, w1, w3, w2)
        x_ctx = _ln2(alpha, x1, n_lat_tiles, ffn_c, modsel, 1, row(ln2_g[l]), row(ln2_b[l]), tm)
    return x_lat
```
